```python
import math
import jax, jax.numpy as jnp
from jax import lax
import numpy as np

D_MODEL = 1024
BATCH = 4
SEQ = 8192
DEPTH = 2
DEC_BATCH = 1
DEC_SEQ = 16384
PAST_LEN = 128

N_MIXERS = 2
N_POOL_LAYERS = (DEPTH + 1) // 2
N_RWKV_LAYERS = DEPTH // 2
NORM_EPS = 1e-6
POOL_WINDOWS = (2, 4, 8, 16)
POOL_GROUP = D_MODEL // len(POOL_WINDOWS)
RWKV_HEAD = 64
RWKV_HEADS = D_MODEL // RWKV_HEAD
DECAY_LORA = 64
ICL_LORA = 64
GATE_LORA = 128
DECAY_SCALE = math.exp(-0.5)
LNX_EPS = 64e-5
PEER_HEADS = 8
PEER_KEYS = 128
PEER_EXPERTS = PEER_KEYS * PEER_KEYS
PEER_DK = 256
PEER_TOPK = 16
PEER_CHUNK = 128

kernel_name = 'pool_rwkv7_peer_bidir_encoder'


def rmsnorm(x, g):
    xf = x.astype(jnp.float32)
    y = xf * lax.rsqrt(jnp.mean(xf * xf, axis=-1, keepdims=True) + NORM_EPS)
    return (y * g.astype(jnp.float32)).astype(x.dtype)


def pool_mixer(x, pool_w, pool_scale):
    B, S, D = x.shape
    xf = x.astype(jnp.float32)
    cs = jnp.concatenate([jnp.zeros((B, 1, D), jnp.float32), jnp.cumsum(xf, axis=1)], axis=1)
    t = jnp.arange(S)
    outs = []
    for gi, w in enumerate(POOL_WINDOWS):
        sl = slice(gi * POOL_GROUP, (gi + 1) * POOL_GROUP)
        lo = jnp.clip(t - w // 2, 0, S)
        hi = jnp.clip(t + w // 2, 0, S)
        csg = cs[..., sl]
        cnt = (hi - lo).astype(jnp.float32)[None, :, None]
        mean = (csg[:, hi] - csg[:, lo]) / cnt
        outs.append((mean - xf[..., sl]) @ pool_w[gi].astype(jnp.float32))
    y = jnp.concatenate(outs, axis=-1) * pool_scale.astype(jnp.float32)
    return y.astype(x.dtype)


def wkv_scan(r, w, k, v, kk, a, reverse):
    S_len, B, H, N = r.shape
    s0 = jnp.zeros((B, H, N, N), jnp.float32)

    def step(st, inp):
        r_t, w_t, k_t, v_t, kk_t, a_t = inp
        sa = jnp.einsum('bhij,bhj->bhi', st, -kk_t)
        st = (st * w_t[:, :, None, :] + sa[..., None] * (kk_t * a_t)[:, :, None, :]
              + v_t[..., None] * k_t[:, :, None, :])
        y = jnp.einsum('bhij,bhj->bhi', st, r_t)
        return st, y

    _, ys = lax.scan(step, s0, (r, w, k, v, kk, a), reverse=reverse)
    return ys


def rwkv_mixer(x, mu, w_rkv, w_o, dec_w0, dec_w1, dec_w2, icl_a0, icl_a1, icl_a2,
               gate_g1, gate_g2, k_k, k_a, r_k, lnx_g, lnx_b):
    B, S, D = x.shape
    H, N = RWKV_HEADS, RWKV_HEAD
    xf = x.astype(jnp.float32)
    zero = jnp.zeros((B, 1, D), jnp.float32)
    x_prev = jnp.concatenate([zero, xf[:, :-1]], axis=1)
    x_next = jnp.concatenate([xf[:, 1:], zero], axis=1)
    xx = 0.5 * (x_prev + x_next) - xf
    xr, xw, xk, xv, xa, xg = (xf + xx * mu[i] for i in range(6))
    r = xr @ w_rkv[0]
    k = xk @ w_rkv[1]
    v = xv @ w_rkv[2]
    g = jax.nn.sigmoid(xg @ gate_g1) @ gate_g2
    kk = (k * k_k).reshape(B, S, H, N)
    kk = kk / jnp.maximum(jnp.linalg.norm(kk, axis=-1, keepdims=True), 1e-12)

    def tm(t):
        return t.reshape(B, S, H, N).transpose(1, 0, 2, 3)

    y = 0.0
    for d, rev in enumerate((False, True)):
        z = dec_w0[d] + jnp.tanh(xw @ dec_w1[d]) @ dec_w2[d]
        decay = jnp.exp(-DECAY_SCALE * jax.nn.sigmoid(z))
        a = jax.nn.sigmoid(icl_a0[d] + (xa @ icl_a1[d]) @ icl_a2[d])
        kd = k * (1.0 + (a - 1.0) * k_a)
        y = y + wkv_scan(tm(r), tm(decay), tm(kd), tm(v), tm(kk), tm(a), rev)
    y = y.transpose(1, 0, 2, 3)
    mean = jnp.mean(y, axis=-1, keepdims=True)
    var = jnp.mean((y - mean) ** 2, axis=-1, keepdims=True)
    yn = ((y - mean) * lax.rsqrt(var + LNX_EPS)).reshape(B, S, D) * lnx_g + lnx_b
    rh, kh, vh = r.reshape(B, S, H, N), k.reshape(B, S, H, N), v.reshape(B, S, H, N)
    bonus = (jnp.sum(rh * kh * r_k, axis=-1, keepdims=True) * vh).reshape(B, S, D)
    out = ((yn + bonus) * g) @ w_o
    return out.astype(x.dtype)


def peer_ffn(h, q_w, subkeys, u_tab, v_tab):
    B, S, D = h.shape
    tok = h.reshape(-1, PEER_CHUNK, D)
    qw = q_w.astype(jnp.float32)
    sk = subkeys.astype(jnp.float32)

    def chunk(hc):
        hf = hc.astype(jnp.float32)
        q = (hf @ qw).reshape(PEER_CHUNK, PEER_HEADS, 2, PEER_DK // 2)
        s = jnp.einsum('chpd,hpkd->chpk', q, sk)
        s1, i1 = lax.top_k(s[:, :, 0], PEER_TOPK)
        s2, i2 = lax.top_k(s[:, :, 1], PEER_TOPK)
        cand = (s1[..., :, None] + s2[..., None, :]).reshape(PEER_CHUNK, PEER_HEADS, -1)
        cidx = (i1[..., :, None] * PEER_KEYS + i2[..., None, :]).reshape(PEER_CHUNK, PEER_HEADS, -1)
        top, pos = lax.top_k(cand, PEER_TOPK)
        idx = jnp.take_along_axis(cidx, pos, axis=-1)
        gate = jax.nn.softmax(top, axis=-1)
        ue = u_tab[idx].astype(jnp.float32)
        ve = v_tab[idx].astype(jnp.float32)
        act = jax.nn.gelu(jnp.einsum('chkd,cd->chk', ue, hf), approximate=False)
        return jnp.einsum('chk,chkd->cd', gate * act, ve).astype(hc.dtype)

    out = lax.map(chunk, tok)
    return out.reshape(B, S, D)


def trunk(x, norm_mix, norm_ffn, norm_final, pool_w, pool_scale,
          rwkv_mu, rwkv_w_rkv, rwkv_w_o, rwkv_dec_w0, rwkv_dec_w1, rwkv_dec_w2,
          rwkv_icl_a0, rwkv_icl_a1, rwkv_icl_a2, rwkv_gate_g1, rwkv_gate_g2,
          rwkv_k_k, rwkv_k_a, rwkv_r_k, rwkv_lnx_g, rwkv_lnx_b,
          peer_q, peer_subkeys, peer_u, peer_v):
    h = x
    for i in range(DEPTH):
        u = rmsnorm(h, norm_mix[i])
        j = i // N_MIXERS
        if i % N_MIXERS == 0:
            m = pool_mixer(u, pool_w[j], pool_scale[j])
        else:
            m = rwkv_mixer(u, rwkv_mu[j], rwkv_w_rkv[j], rwkv_w_o[j], rwkv_dec_w0[j],
                           rwkv_dec_w1[j], rwkv_dec_w2[j], rwkv_icl_a0[j], rwkv_icl_a1[j],
                           rwkv_icl_a2[j], rwkv_gate_g1[j], rwkv_gate_g2[j], rwkv_k_k[j],
                           rwkv_k_a[j], rwkv_r_k[j], rwkv_lnx_g[j], rwkv_lnx_b[j])
        h = h + m
        h = h + peer_ffn(rmsnorm(h, norm_ffn[i]), peer_q[i], peer_subkeys[i], peer_u[i], peer_v[i])
    return rmsnorm(h, norm_final)


def setup_inputs(seed: int = 0) -> dict:
    key = jax.random.key(seed)
    ks = iter(jax.random.split(key, 40))
    D = D_MODEL
    nrm = lambda shape, scale: jax.random.normal(next(ks), shape, jnp.float32) * scale
    NA, NB = N_POOL_LAYERS, N_RWKV_LAYERS
    return {
        'x_prompt': nrm((BATCH, SEQ, D), 1.0),
        'x_sample': nrm((DEC_BATCH, DEC_SEQ, D), 1.0),
        'norm_mix': 1.0 + nrm((DEPTH, D), 0.02),
        'norm_ffn': 1.0 + nrm((DEPTH, D), 0.02),
        'norm_final': 1.0 + nrm((D,), 0.02),
        'pool_w': nrm((NA, len(POOL_WINDOWS), POOL_GROUP, POOL_GROUP), POOL_GROUP ** -0.5),
        'pool_scale': 1.0 + nrm((NA, D), 0.02),
        'rwkv_mu': 0.5 + nrm((NB, 6, D), 0.1),
        'rwkv_w_rkv': nrm((NB, 3, D, D), D ** -0.5),
        'rwkv_w_o': nrm((NB, D, D), D ** -0.5),
        'rwkv_dec_w0': nrm((NB, 2, D), 0.5),
        'rwkv_dec_w1': nrm((NB, 2, D, DECAY_LORA), D ** -0.5),
        'rwkv_dec_w2': nrm((NB, 2, DECAY_LORA, D), 0.5 * DECAY_LORA ** -0.5),
        'rwkv_icl_a0': nrm((NB, 2, D), 0.5),
        'rwkv_icl_a1': nrm((NB, 2, D, ICL_LORA), D ** -0.5),
        'rwkv_icl_a2': nrm((NB, 2, ICL_LORA, D), 0.5 * ICL_LORA ** -0.5),
        'rwkv_gate_g1': nrm((NB, D, GATE_LORA), D ** -0.5),
        'rwkv_gate_g2': nrm((NB, GATE_LORA, D), GATE_LORA ** -0.5),
        'rwkv_k_k': 0.85 + nrm((NB, D), 0.05),
        'rwkv_k_a': 1.0 + nrm((NB, D), 0.05),
        'rwkv_r_k': nrm((NB, RWKV_HEADS, RWKV_HEAD), 0.1),
        'rwkv_lnx_g': 1.0 + nrm((NB, D), 0.02),
        'rwkv_lnx_b': nrm((NB, D), 0.02),
        'peer_q': nrm((DEPTH, D, PEER_HEADS * PEER_DK), D ** -0.5),
        'peer_subkeys': nrm((DEPTH, PEER_HEADS, 2, PEER_KEYS, PEER_DK // 2), (PEER_DK // 2) ** -0.5),
        'peer_u': nrm((DEPTH, PEER_EXPERTS, D), D ** -0.5),
        'peer_v': nrm((DEPTH, PEER_EXPERTS, D), PEER_HEADS ** -0.5),
    }


def reference(x_prompt, x_sample, norm_mix, norm_ffn, norm_final, pool_w, pool_scale,
              rwkv_mu, rwkv_w_rkv, rwkv_w_o, rwkv_dec_w0, rwkv_dec_w1, rwkv_dec_w2,
              rwkv_icl_a0, rwkv_icl_a1, rwkv_icl_a2, rwkv_gate_g1, rwkv_gate_g2,
              rwkv_k_k, rwkv_k_a, rwkv_r_k, rwkv_lnx_g, rwkv_lnx_b,
              peer_q, peer_subkeys, peer_u, peer_v):
    weights = (norm_mix, norm_ffn, norm_final, pool_w, pool_scale,
               rwkv_mu, rwkv_w_rkv, rwkv_w_o, rwkv_dec_w0, rwkv_dec_w1, rwkv_dec_w2,
               rwkv_icl_a0, rwkv_icl_a1, rwkv_icl_a2, rwkv_gate_g1, rwkv_gate_g2,
               rwkv_k_k, rwkv_k_a, rwkv_r_k, rwkv_lnx_g, rwkv_lnx_b,
               peer_q, peer_subkeys, peer_u, peer_v)
    y_prompt = trunk(x_prompt, *weights)
    y_sample = trunk(x_sample, *weights)
    return (y_prompt, y_sample)
```

```python
import functools
import math

import numpy as np
import jax
import jax.numpy as jnp
from jax import lax
from jax.experimental import pallas as pl
from jax.experimental.pallas import tpu as pltpu

F32 = jnp.float32
BF16 = jnp.bfloat16

D_MODEL = 1024
NORM_EPS = 1e-6
POOL_WINDOWS = (2, 4, 8, 16)
POOL_GROUP = D_MODEL // len(POOL_WINDOWS)
RWKV_HEAD = 64
RWKV_HEADS = D_MODEL // RWKV_HEAD
DECAY_SCALE = math.exp(-0.5)
LNX_EPS = 64e-5
PEER_HEADS = 8
PEER_KEYS = 128
PEER_EXPERTS = PEER_KEYS * PEER_KEYS
PEER_TOPK = 16
INV_SQRT2 = 0.7071067811865476

HALO = 8
VMEM_LIMIT = 56 * 1024 * 1024

POOL_TILE = 512
PROJ_TILE = 256
POST_TILE = 256
ROUTER_TILE = 256
DENSE_TILE = 512
DENSE_EBLK = 1024
SCAN_CHUNK = 64
NTOP = PEER_TOPK + 1
CAND_COUNTS = tuple(min(NTOP, NTOP // (i + 1)) for i in range(NTOP))
CAND_ROWS = -(-sum(CAND_COUNTS) // 8) * 8


def _params(sem):
    return pltpu.CompilerParams(dimension_semantics=sem, vmem_limit_bytes=VMEM_LIMIT)


def _split(w):
    hi = w.astype(BF16)
    lo = (w - hi.astype(F32)).astype(BF16)
    return hi, lo


def _dot(a, b, dims=(((1,), (0,)), ((), ()))):
    return lax.dot_general(a, b, dims, preferred_element_type=F32)


def _dot3(a, b_hi, b_lo, dims=(((1,), (0,)), ((), ()))):
    a_hi, a_lo = _split(a)
    return _dot(a_hi, b_hi, dims) + (_dot(a_lo, b_hi, dims) + _dot(a_hi, b_lo, dims))


def _dot2(a, b_exact, dims=(((1,), (0,)), ((), ()))):
    a_hi, a_lo = _split(a)
    return _dot(a_hi, b_exact, dims) + _dot(a_lo, b_exact, dims)


def _dot3f(a, b, dims=(((1,), (0,)), ((), ()))):
    b_hi, b_lo = _split(b)
    return _dot3(a, b_hi, b_lo, dims)


NT = (((1,), (1,)), ((), ()))
TN = (((0,), (0,)), ((), ()))


def _rmsnorm(x, g):
    return x * lax.rsqrt(jnp.mean(x * x, axis=-1, keepdims=True) + NORM_EPS) * g


def _tile_flags(seq_lens, tile):
    first, last = [], []
    for s in seq_lens:
        n = s // tile
        assert n * tile == s
        first += [1] + [0] * (n - 1)
        last += [0] * (n - 1) + [1]
    return jnp.asarray(np.array([first, last], np.int32))


def _halo_specs(tile, n_tok):
    r = tile // HALO
    nb = n_tok // HALO
    return [
        pl.BlockSpec((tile, D_MODEL), lambda i, f: (i, 0)),
        pl.BlockSpec((HALO, D_MODEL), lambda i, f: (jnp.maximum(i * r - 1, 0), 0)),
        pl.BlockSpec((HALO, D_MODEL), lambda i, f: (jnp.minimum((i + 1) * r, nb - 1), 0)),
    ]


def _fill_extended(xe_ref, u, up, un, first, last, tile):
    xe_ref[0:HALO, :] = jnp.where(first != 0, 0.0, up)
    xe_ref[HALO:HALO + tile, :] = u
    xe_ref[HALO + tile:2 * HALO + tile, :] = jnp.where(last != 0, 0.0, un)


def _const_spec(shape):
    nd = len(shape)
    return pl.BlockSpec(shape, lambda *_: (0,) * nd, pipeline_mode=pl.Buffered(1))


def _pool_kernel(flags_ref, x_ref, xp_ref, xn_ref, g_ref, pwh_ref, pwl_ref, ps_ref, o_ref, xe_ref):
    i = pl.program_id(0)
    tile = x_ref.shape[0]
    first = flags_ref[0, i]
    last = flags_ref[1, i]
    g = g_ref[...]
    x = x_ref[...]
    u = _rmsnorm(x, g)
    _fill_extended(xe_ref, u, _rmsnorm(xp_ref[...], g), _rmsnorm(xn_ref[...], g), first, last, tile)
    row = lax.broadcasted_iota(jnp.int32, (tile, 1), 0)
    for gi, w in enumerate(POOL_WINDOWS):
        sl = slice(gi * POOL_GROUP, (gi + 1) * POOL_GROUP)
        acc = xe_ref[pl.ds(HALO - w // 2, tile), sl]
        for j in range(-w // 2 + 1, w // 2):
            acc = acc + xe_ref[pl.ds(HALO + j, tile), sl]
        lo_clip = jnp.where(first != 0, jnp.maximum(w // 2 - row, 0), 0)
        hi_clip = jnp.where(last != 0, jnp.maximum(row + w // 2 - tile, 0), 0)
        cnt = (w - lo_clip - hi_clip).astype(F32)
        diff = acc / cnt - u[:, sl]
        y = _dot3(diff, pwh_ref[gi], pwl_ref[gi])
        o_ref[:, sl] = x[:, sl] + y * ps_ref[:, sl]


def _pool_layer(h, flags, norm_g, pool_w, pool_scale):
    n_tok = h.shape[0]
    tile = POOL_TILE
    pwh, pwl = _split(pool_w)
    grid_spec = pltpu.PrefetchScalarGridSpec(
        num_scalar_prefetch=1,
        grid=(n_tok // tile,),
        in_specs=_halo_specs(tile, n_tok) + [
            _const_spec((1, D_MODEL)), _const_spec(pwh.shape), _const_spec(pwl.shape),
            _const_spec((1, D_MODEL)),
        ],
        out_specs=pl.BlockSpec((tile, D_MODEL), lambda i, f: (i, 0)),
        scratch_shapes=[pltpu.VMEM((tile + 2 * HALO, D_MODEL), F32)],
    )
    return pl.pallas_call(
        _pool_kernel,
        grid_spec=grid_spec,
        out_shape=jax.ShapeDtypeStruct((n_tok, D_MODEL), F32),
        compiler_params=_params(("parallel",)),
        name="pool",
    )(flags, h, h, h, norm_g.reshape(1, -1), pwh, pwl, pool_scale.reshape(1, -1))


def _top_values(cur, n, out_ref, slot):
    for r in range(n):
        mx = jnp.max(cur, axis=0, keepdims=True)
        out_ref[slot, r:r + 1, :] = mx
        cur = jnp.where(cur == mx, -jnp.inf, cur)


def _router_kernel(h_ref, g_ref, qwh_ref, qwl_ref, skh_ref, skl_ref,
                   hnT_ref, thr_ref, e1_ref, s2_ref, e2_ref, s_ref, tops_ref, cand_ref):
    hn = _rmsnorm(h_ref[...], g_ref[...])
    hnT_ref[...] = hn.T.astype(BF16)
    q = _dot3(hn, qwh_ref[...], qwl_ref[...])
    for hp in range(2 * PEER_HEADS):
        qs = q[:, hp * PEER_KEYS:(hp + 1) * PEER_KEYS]
        qh, ql = _split(qs)
        skh = skh_ref[hp]
        s_ref[hp] = _dot(skh, qh, NT) + (_dot(skl_ref[hp], qh, NT) + _dot(skh, ql, NT))
    for h in range(PEER_HEADS):
        s1 = s_ref[2 * h]
        s2 = s_ref[2 * h + 1]
        _top_values(s1, NTOP, tops_ref, 0)
        _top_values(s2, NTOP, tops_ref, 1)
        cand_ref[...] = jnp.full(cand_ref.shape, -jnp.inf, F32)
        off = 0
        for i, n in enumerate(CAND_COUNTS):
            cand_ref[0, off:off + n, :] = tops_ref[0, i:i + 1, :] + tops_ref[1, 0:n, :]
            off += n
        m1 = tops_ref[0, 0:1, :]
        m2 = tops_ref[1, 0:1, :]
        _top_values(cand_ref[0], NTOP, tops_ref, 2)
        top = tops_ref[2, 0:PEER_TOPK, :]
        z = jnp.sum(jnp.exp(top - top[0:1, :]), axis=0, keepdims=True)
        tau = 0.5 * (tops_ref[2, PEER_TOPK - 1:PEER_TOPK, :] + tops_ref[2, PEER_TOPK:PEER_TOPK + 1, :])
        thr_ref[h] = tau - s1
        e1_ref[h] = jnp.exp(s1 - m1) * (1.0 / z)
        s2_ref[h] = s2
        e2_ref[h] = jnp.exp(s2 - m2)


def _peer_router(h, norm_g, q_w, subkeys):
    n_tok = h.shape[0]
    tile = ROUTER_TILE
    qwh, qwl = _split(q_w)
    sk = subkeys.reshape(2 * PEER_HEADS, PEER_KEYS, PEER_KEYS)
    skh, skl = _split(sk)
    per_head = pl.BlockSpec((PEER_HEADS, PEER_KEYS, tile), lambda i: (0, 0, i))
    per_head_shape = jax.ShapeDtypeStruct((PEER_HEADS, PEER_KEYS, n_tok), F32)
    return pl.pallas_call(
        _router_kernel,
        grid=(n_tok // tile,),
        in_specs=[
            pl.BlockSpec((tile, D_MODEL), lambda i: (i, 0)),
            _const_spec((1, D_MODEL)),
            _const_spec(qwh.shape), _const_spec(qwl.shape),
            _const_spec(skh.shape), _const_spec(skl.shape),
        ],
        out_specs=[pl.BlockSpec((D_MODEL, tile), lambda i: (0, i))] + [per_head] * 4,
        out_shape=[jax.ShapeDtypeStruct((D_MODEL, n_tok), BF16)] + [per_head_shape] * 4,
        scratch_shapes=[pltpu.VMEM((2 * PEER_HEADS, PEER_KEYS, tile), F32),
                        pltpu.VMEM((3, 24, tile), F32), pltpu.VMEM((1, CAND_ROWS, tile), F32)],
        compiler_params=_params(("parallel",)),
        name="peer_router",
    )(h, norm_g.reshape(1, -1), qwh, qwl, skh, skl)


def _dense_kernel(h_ref, hnT_ref, thr_ref, e1_ref, s2_ref, e2_ref, u_ref, vT_ref, gf_ref, o_ref,
                  acc_ref, act_ref, p_ref, *, final_norm):
    j = pl.program_id(1)
    eblk, tile = act_ref.shape

    @pl.when(j == 0)
    def _():
        acc_ref[...] = jnp.zeros(acc_ref.shape, F32)

    act_ref[...] = _dot(u_ref[...], hnT_ref[...])

    for al in range(eblk // PEER_KEYS):
        rows = slice(al * PEER_KEYS, (al + 1) * PEER_KEYS)

        def chunk(tc, carry):
            col = pl.multiple_of(tc * 128, 128)
            gate = jnp.zeros((PEER_KEYS, 128), F32)
            for h in range(PEER_HEADS):
                thr_row = thr_ref[h, al:al + 1, pl.ds(col, 128)]
                e1_row = e1_ref[h, al:al + 1, pl.ds(col, 128)]
                s2 = s2_ref[h, :, pl.ds(col, 128)]
                e2 = e2_ref[h, :, pl.ds(col, 128)]
                gate = gate + jnp.where(s2 >= thr_row, e2 * e1_row, 0.0)
            x = act_ref[rows, pl.ds(col, 128)]
            gelu = 0.5 * x * (1.0 + lax.erf(x * INV_SQRT2))
            p_ref[rows, pl.ds(col, 128)] = (gate * gelu).astype(BF16)
            return carry

        lax.fori_loop(0, tile // 128, chunk, 0)
    acc_ref[...] += _dot(vT_ref[...], p_ref[...])

    @pl.when(j == pl.num_programs(1) - 1)
    def _():
        out = h_ref[...] + acc_ref[...].T
        if final_norm:
            out = _rmsnorm(out, gf_ref[...])
        o_ref[...] = out


def _peer_dense(h, hnT, thr, e1, s2, e2, u_bf, vT_bf, final_g, final_norm):
    n_tok = h.shape[0]
    tile, eblk = DENSE_TILE, DENSE_EBLK
    rows = pl.BlockSpec((PEER_HEADS, eblk // PEER_KEYS, tile), lambda i, j: (0, j, i))
    full = pl.BlockSpec((PEER_HEADS, PEER_KEYS, tile), lambda i, j: (0, 0, i))
    return pl.pallas_call(
        functools.partial(_dense_kernel, final_norm=final_norm),
        grid=(n_tok // tile, PEER_EXPERTS // eblk),
        in_specs=[
            pl.BlockSpec((tile, D_MODEL), lambda i, j: (i, 0)),
            pl.BlockSpec((D_MODEL, tile), lambda i, j: (0, i)),
            rows, rows, full, full,
            pl.BlockSpec((eblk, D_MODEL), lambda i, j: (j, 0)),
            pl.BlockSpec((D_MODEL, eblk), lambda i, j: (0, j)),
            _const_spec((1, D_MODEL)),
        ],
        out_specs=pl.BlockSpec((tile, D_MODEL), lambda i, j: (i, 0)),
        out_shape=jax.ShapeDtypeStruct((n_tok, D_MODEL), F32),
        scratch_shapes=[
            pltpu.VMEM((D_MODEL, tile), F32),
            pltpu.VMEM((eblk, tile), F32),
            pltpu.VMEM((eblk, tile), BF16),
        ],
        compiler_params=_params(("parallel", "arbitrary")),
        name="peer_dense",
    )(h, hnT, thr, e1, s2, e2, u_bf, vT_bf, final_g.reshape(1, -1))


def _peer_layer(h, norm_g, q_w, subkeys, u_tab, v_tab, final_g, final_norm):
    hnT, thr, e1, s2, e2 = _peer_router(h, norm_g, q_w, subkeys)
    return _peer_dense(h, hnT, thr, e1, s2, e2, u_tab.astype(BF16), v_tab.T.astype(BF16), final_g, final_norm)


def _head_sum(x, ind_ref, indT_ref):
    return _dot2(_dot2(x, ind_ref[...]), indT_ref[...])


def _proj_kernel(flags_ref, x_ref, xp_ref, xn_ref, g_ref, mu_ref, wh_ref, wl_ref,
                 dw0_ref, dw1_ref, dw2_ref, a0_ref, a1_ref, a2_ref, g1_ref, g2_ref,
                 kk_ref_w, ka_ref, ind_ref, indT_ref,
                 r_ref, k_ref, v_ref, gate_ref, kko_ref, lw_ref, kd_ref, beta_ref, xe_ref):
    i = pl.program_id(0)
    tile = x_ref.shape[0]
    first = flags_ref[0, i]
    last = flags_ref[1, i]
    g = g_ref[...]
    u = _rmsnorm(x_ref[...], g)
    _fill_extended(xe_ref, u, _rmsnorm(xp_ref[...], g), _rmsnorm(xn_ref[...], g), first, last, tile)
    xx = 0.5 * (xe_ref[pl.ds(HALO - 1, tile), :] + xe_ref[pl.ds(HALO + 1, tile), :]) - u
    xr, xw, xk, xv, xa, xg = (u + xx * mu_ref[m:m + 1, :] for m in range(6))
    r = _dot3(xr, wh_ref[0], wl_ref[0])
    k = _dot3(xk, wh_ref[1], wl_ref[1])
    v = _dot3(xv, wh_ref[2], wl_ref[2])
    r_ref[...] = r
    k_ref[...] = k
    v_ref[...] = v
    gate_ref[...] = _dot3f(jax.nn.sigmoid(_dot3f(xg, g1_ref[...])), g2_ref[...])
    kk = k * kk_ref_w[...]
    nrm = jnp.sqrt(_head_sum(kk * kk, ind_ref, indT_ref))
    kk = kk / jnp.maximum(nrm, 1e-12)
    kko_ref[...] = kk
    for d in range(2):
        z = dw0_ref[d:d + 1, :] + _dot3f(jnp.tanh(_dot3f(xw, dw1_ref[d])), dw2_ref[d])
        lw_ref[d] = -DECAY_SCALE * jax.nn.sigmoid(z)
        a = jax.nn.sigmoid(a0_ref[d:d + 1, :] + _dot3f(_dot3f(xa, a1_ref[d]), a2_ref[d]))
        kd_ref[d] = k * (1.0 + (a - 1.0) * ka_ref[...])
        beta_ref[d] = kk * a


def _head_indicator():
    ind = (np.arange(D_MODEL)[:, None] // RWKV_HEAD == np.arange(RWKV_HEADS)[None, :]).astype(np.float32)
    pad = np.zeros((D_MODEL, 128), np.float32)
    pad[:, :RWKV_HEADS] = ind
    return jnp.asarray(pad, BF16), jnp.asarray(pad.T, BF16)


def _rwkv_proj(h, flags, norm_g, mu, w_rkv, dec_w0, dec_w1, dec_w2, icl_a0, icl_a1, icl_a2,
               gate_g1, gate_g2, k_k, k_a):
    n_tok = h.shape[0]
    tile = PROJ_TILE
    wh, wl = _split(w_rkv)
    ind, indT = _head_indicator()
    consts = [norm_g.reshape(1, -1), mu, wh, wl, dec_w0, dec_w1, dec_w2, icl_a0, icl_a1, icl_a2,
              gate_g1, gate_g2, k_k.reshape(1, -1), k_a.reshape(1, -1), ind, indT]

    tok = pl.BlockSpec((tile, D_MODEL), lambda i, f: (i, 0))
    tok2 = pl.BlockSpec((2, tile, D_MODEL), lambda i, f: (0, i, 0))
    one = jax.ShapeDtypeStruct((n_tok, D_MODEL), F32)
    two = jax.ShapeDtypeStruct((2, n_tok, D_MODEL), F32)
    grid_spec = pltpu.PrefetchScalarGridSpec(
        num_scalar_prefetch=1,
        grid=(n_tok // tile,),
        in_specs=_halo_specs(tile, n_tok) + [_const_spec(a.shape) for a in consts],
        out_specs=[tok] * 5 + [tok2] * 3,
        scratch_shapes=[pltpu.VMEM((tile + 2 * HALO, D_MODEL), F32)],
    )
    return pl.pallas_call(
        _proj_kernel,
        grid_spec=grid_spec,
        out_shape=[one] * 5 + [two] * 3,
        compiler_params=_params(("parallel",)),
        name="rwkv_proj",
    )(flags, h, h, h, *consts)


def _scan_direction(rev, reset, r_ref, lw_ref, kd_ref, v_ref, kk_ref, beta_ref, y_ref, st_ref, tri):
    L = r_ref.shape[0]
    N = RWKV_HEAD
    incl, strict, eye = tri
    lw = lw_ref[...]
    lw_hi = lw.astype(BF16)
    lw_r = lw - lw_hi.astype(F32)
    lw_mid = lw_r.astype(BF16)
    lw_lo = (lw_r - lw_mid.astype(F32)).astype(BF16)
    inc_bf = incl.astype(BF16)
    cs = _dot(inc_bf, lw_hi) + (_dot(inc_bf, lw_mid) + _dot(inc_bf, lw_lo))
    end = L - 1 if not rev else 0
    cs_end = cs[end:end + 1, :]
    g_in = jnp.exp(cs)
    g_prev = jnp.exp(cs - lw)
    g_inv = jnp.exp(-cs)
    g_rest = jnp.exp(cs_end - cs)
    kk = kk_ref[...]
    beta = beta_ref[...]
    kd = kd_ref[...]
    alpha_b = -kk * g_prev
    r_b = r_ref[...] * g_in
    beta_t = beta * g_inv
    k_t = kd * g_inv
    beta_h = beta * g_rest
    k_h = kd * g_rest
    g_end = jnp.exp(cs_end)
    v = v_ref[...]
    for h in range(RWKV_HEADS):
        sl = slice(h * N, (h + 1) * N)
        t0 = jnp.where(reset != 0, 0.0, st_ref[h])
        x2 = jnp.concatenate([alpha_b[:, sl], r_b[:, sl]], axis=0)
        y2 = jnp.concatenate([beta_t[:, sl], k_t[:, sl]], axis=0)
        aa = _dot3f(x2, y2, NT)
        xt = _dot3f(x2, t0)
        a_ab = jnp.where(strict, aa[0:L, 0:L], 0.0)
        a_ak = jnp.where(strict, aa[0:L, L:2 * L], 0.0)
        a_rb = jnp.where(incl, aa[L:2 * L, 0:L], 0.0)
        a_rk = jnp.where(incl, aa[L:2 * L, L:2 * L], 0.0)
        m = eye + a_ab
        ap = a_ab
        for _ in range(int(math.log2(L)) - 1):
            ap = _dot3f(ap, ap)
            m = m + _dot3f(m, ap)
        vh = v[:, sl]
        rhs = xt[0:L] + _dot3f(a_ak, vh)
        u = _dot3f(m, rhs)
        uv = jnp.concatenate([u, vh], axis=0)
        y = xt[L:2 * L] + _dot3f(jnp.concatenate([a_rb, a_rk], axis=1), uv)
        y_ref[:, sl] = y
        bk = jnp.concatenate([beta_h[:, sl], k_h[:, sl]], axis=0)
        st_ref[h] = t0 * g_end[:, sl].T + _dot3f(bk, uv, TN)


def _scan_kernel(flags_ref, rf, lwf, kdf, vf, kkf, bf, rb, lwb, kdb, vb, kkb, bb, yf_ref, yb_ref, stf_ref, stb_ref):
    c = pl.program_id(0)
    n = pl.num_programs(0)
    L = rf.shape[0]
    row = lax.broadcasted_iota(jnp.int32, (L, L), 0)
    col = lax.broadcasted_iota(jnp.int32, (L, L), 1)
    eye = (row == col).astype(F32)
    _scan_direction(False, flags_ref[0, c], rf, lwf, kdf, vf, kkf, bf, yf_ref, stf_ref,
                    (row >= col, row > col, eye))
    _scan_direction(True, flags_ref[1, n - 1 - c], rb, lwb, kdb, vb, kkb, bb, yb_ref, stb_ref,
                    (row <= col, row < col, eye))


def _rwkv_scan(flags, r, lw, kd, v, kk, beta):
    n_tok = r.shape[0]
    L = SCAN_CHUNK
    n = n_tok // L
    fwd = pl.BlockSpec((L, D_MODEL), lambda c, f: (c, 0))
    bwd = pl.BlockSpec((L, D_MODEL), lambda c, f: (n - 1 - c, 0))

    def dspec(d, rev):
        return pl.BlockSpec((None, L, D_MODEL), (lambda c, f: (d, n - 1 - c, 0)) if rev else (lambda c, f: (d, c, 0)))

    grid_spec = pltpu.PrefetchScalarGridSpec(
        num_scalar_prefetch=1,
        grid=(n,),
        in_specs=[fwd, dspec(0, False), dspec(0, False), fwd, fwd, dspec(0, False),
                  bwd, dspec(1, True), dspec(1, True), bwd, bwd, dspec(1, True)],
        out_specs=[fwd, bwd],
        scratch_shapes=[pltpu.VMEM((RWKV_HEADS, RWKV_HEAD, RWKV_HEAD), F32)] * 2,
    )
    one = jax.ShapeDtypeStruct((n_tok, D_MODEL), F32)
    return pl.pallas_call(
        _scan_kernel,
        grid_spec=grid_spec,
        out_shape=[one, one],
        compiler_params=_params(("arbitrary",)),
        name="rwkv_scan",
    )(flags, r, lw, kd, v, kk, beta, r, lw, kd, v, kk, beta)


def _post_kernel(h_ref, yf_ref, yb_ref, r_ref, k_ref, v_ref, g_ref, rk_ref, lg_ref, lb_ref,
                 woh_ref, wol_ref, ind_ref, indT_ref, o_ref):
    y = yf_ref[...] + yb_ref[...]
    inv_n = 1.0 / RWKV_HEAD
    mean = _head_sum(y, ind_ref, indT_ref) * inv_n
    yc = y - mean
    var = _head_sum(yc * yc, ind_ref, indT_ref) * inv_n
    yn = yc * lax.rsqrt(var + LNX_EPS) * lg_ref[...] + lb_ref[...]
    bonus = _head_sum(r_ref[...] * k_ref[...] * rk_ref[...], ind_ref, indT_ref) * v_ref[...]
    out = _dot3((yn + bonus) * g_ref[...], woh_ref[...], wol_ref[...])
    o_ref[...] = h_ref[...] + out


def _rwkv_post(h, yf, yb, r, k, v, g, r_k, lnx_g, lnx_b, w_o):
    n_tok = h.shape[0]
    tile = POST_TILE
    woh, wol = _split(w_o)
    ind, indT = _head_indicator()
    tok = pl.BlockSpec((tile, D_MODEL), lambda i: (i, 0))
    consts = [r_k.reshape(1, -1), lnx_g.reshape(1, -1), lnx_b.reshape(1, -1), woh, wol, ind, indT]
    return pl.pallas_call(
        _post_kernel,
        grid=(n_tok // tile,),
        in_specs=[tok] * 7 + [_const_spec(a.shape) for a in consts],
        out_specs=tok,
        out_shape=jax.ShapeDtypeStruct((n_tok, D_MODEL), F32),
        compiler_params=_params(("parallel",)),
        name="rwkv_post",
    )(h, yf, yb, r, k, v, g, *consts)


def _rwkv_layer(h, seq_lens, norm_g, mu, w_rkv, w_o, dec_w0, dec_w1, dec_w2, icl_a0, icl_a1, icl_a2,
                gate_g1, gate_g2, k_k, k_a, r_k, lnx_g, lnx_b):
    r, k, v, g, kk, lw, kd, beta = _rwkv_proj(
        h, _tile_flags(seq_lens, PROJ_TILE), norm_g, mu, w_rkv, dec_w0, dec_w1, dec_w2,
        icl_a0, icl_a1, icl_a2, gate_g1, gate_g2, k_k, k_a)
    yf, yb = _rwkv_scan(_tile_flags(seq_lens, SCAN_CHUNK), r, lw, kd, v, kk, beta)
    return _rwkv_post(h, yf, yb, r, k, v, g, r_k, lnx_g, lnx_b, w_o)


def _trunk(h, seq_lens, norm_mix, norm_ffn, norm_final, pool_w, pool_scale,
           rwkv_mu, rwkv_w_rkv, rwkv_w_o, rwkv_dec_w0, rwkv_dec_w1, rwkv_dec_w2,
           rwkv_icl_a0, rwkv_icl_a1, rwkv_icl_a2, rwkv_gate_g1, rwkv_gate_g2,
           rwkv_k_k, rwkv_k_a, rwkv_r_k, rwkv_lnx_g, rwkv_lnx_b,
           peer_q, peer_subkeys, peer_u, peer_v):
    depth = norm_mix.shape[0]
    for i in range(depth):
        j = i // 2
        if i % 2 == 0:
            h = _pool_layer(h, _tile_flags(seq_lens, POOL_TILE), norm_mix[i], pool_w[j], pool_scale[j])
        else:
            h = _rwkv_layer(h, seq_lens, norm_mix[i], rwkv_mu[j], rwkv_w_rkv[j], rwkv_w_o[j],
                            rwkv_dec_w0[j], rwkv_dec_w1[j], rwkv_dec_w2[j], rwkv_icl_a0[j],
                            rwkv_icl_a1[j], rwkv_icl_a2[j], rwkv_gate_g1[j], rwkv_gate_g2[j],
                            rwkv_k_k[j], rwkv_k_a[j], rwkv_r_k[j].reshape(-1), rwkv_lnx_g[j], rwkv_lnx_b[j])
        h = _peer_layer(h, norm_ffn[i], peer_q[i], peer_subkeys[i], peer_u[i], peer_v[i],
                        norm_final, final_norm=(i == depth - 1))
    return h


def kernel(x_prompt, x_sample, norm_mix, norm_ffn, norm_final, pool_w, pool_scale, rwkv_mu, rwkv_w_rkv, rwkv_w_o, rwkv_dec_w0, rwkv_dec_w1, rwkv_dec_w2, rwkv_icl_a0, rwkv_icl_a1, rwkv_icl_a2, rwkv_gate_g1, rwkv_gate_g2, rwkv_k_k, rwkv_k_a, rwkv_r_k, rwkv_lnx_g, rwkv_lnx_b, peer_q, peer_subkeys, peer_u, peer_v):
    bp, sp, d = x_prompt.shape
    bs, ss, _ = x_sample.shape
    seq_lens = (sp,) * bp + (ss,) * bs
    h = jnp.concatenate([x_prompt.reshape(-1, d), x_sample.reshape(-1, d)], axis=0)
    y = _trunk(h, seq_lens, norm_mix, norm_ffn, norm_final, pool_w, pool_scale,
               rwkv_mu, rwkv_w_rkv, rwkv_w_o, rwkv_dec_w0, rwkv_dec_w1, rwkv_dec_w2,
               rwkv_icl_a0, rwkv_icl_a1, rwkv_icl_a2, rwkv_gate_g1, rwkv_gate_g2,
               rwkv_k_k, rwkv_k_a, rwkv_r_k, rwkv_lnx_g, rwkv_lnx_b,
               peer_q, peer_subkeys, peer_u, peer_v)
    n_p = bp * sp
    return (y[:n_p].reshape(bp, sp, d), y[n_p:].reshape(bs, ss, d))
```

```python
import functools
import math

import numpy as np
import jax
import jax.numpy as jnp
from jax import lax
from jax.experimental import pallas as pl
from jax.experimental.pallas import tpu as pltpu

F32 = jnp.float32
BF16 = jnp.bfloat16

D_MODEL = 1024
NORM_EPS = 1e-6
POOL_WINDOWS = (2, 4, 8, 16)
POOL_GROUP = D_MODEL // len(POOL_WINDOWS)
RWKV_HEAD = 64
RWKV_HEADS = D_MODEL // RWKV_HEAD
DECAY_SCALE = math.exp(-0.5)
LNX_EPS = 64e-5
PEER_HEADS = 8
PEER_KEYS = 128
PEER_EXPERTS = PEER_KEYS * PEER_KEYS
PEER_TOPK = 16
INV_SQRT2 = 0.7071067811865476

HALO = 8
VMEM_LIMIT = 56 * 1024 * 1024

POOL_TILE = 512
PROJ_TILE = 256
POST_TILE = 256
ROUTER_TILE = 256
DENSE_TILE = 512
DENSE_EBLK = 1024
DENSE_SUB = 256
SCAN_CHUNK = 64
NTOP = PEER_TOPK + 1
CAND_COUNTS = tuple(min(NTOP, NTOP // (i + 1)) for i in range(NTOP))
CAND_ROWS = -(-sum(CAND_COUNTS) // 8) * 8


def _params(sem):
    return pltpu.CompilerParams(dimension_semantics=sem, vmem_limit_bytes=VMEM_LIMIT)


def _split(w):
    hi = w.astype(BF16)
    lo = (w - hi.astype(F32)).astype(BF16)
    return hi, lo


def _dot(a, b, dims=(((1,), (0,)), ((), ()))):
    return lax.dot_general(a, b, dims, preferred_element_type=F32)


def _dot3(a, b_hi, b_lo, dims=(((1,), (0,)), ((), ()))):
    a_hi, a_lo = _split(a)
    return _dot(a_hi, b_hi, dims) + (_dot(a_lo, b_hi, dims) + _dot(a_hi, b_lo, dims))


def _dot2(a, b_exact, dims=(((1,), (0,)), ((), ()))):
    a_hi, a_lo = _split(a)
    return _dot(a_hi, b_exact, dims) + _dot(a_lo, b_exact, dims)


def _dot3f(a, b, dims=(((1,), (0,)), ((), ()))):
    b_hi, b_lo = _split(b)
    return _dot3(a, b_hi, b_lo, dims)


def _dot1f(a, b, dims=(((1,), (0,)), ((), ()))):
    return _dot(a.astype(BF16), b.astype(BF16), dims)


NT = (((1,), (1,)), ((), ()))
TN = (((0,), (0,)), ((), ()))


def _rmsnorm(x, g):
    return x * lax.rsqrt(jnp.mean(x * x, axis=-1, keepdims=True) + NORM_EPS) * g


def _tile_flags(seq_lens, tile):
    first, last = [], []
    for s in seq_lens:
        n = s // tile
        assert n * tile == s
        first += [1] + [0] * (n - 1)
        last += [0] * (n - 1) + [1]
    return jnp.asarray(np.array([first, last], np.int32))


def _halo_specs(tile, n_tok):
    r = tile // HALO
    nb = n_tok // HALO
    return [
        pl.BlockSpec((tile, D_MODEL), lambda i, f: (i, 0)),
        pl.BlockSpec((HALO, D_MODEL), lambda i, f: (jnp.maximum(i * r - 1, 0), 0)),
        pl.BlockSpec((HALO, D_MODEL), lambda i, f: (jnp.minimum((i + 1) * r, nb - 1), 0)),
    ]


def _fill_extended(xe_ref, u, up, un, first, last, tile):
    xe_ref[0:HALO, :] = jnp.where(first != 0, 0.0, up)
    xe_ref[HALO:HALO + tile, :] = u
    xe_ref[HALO + tile:2 * HALO + tile, :] = jnp.where(last != 0, 0.0, un)


def _const_spec(shape):
    nd = len(shape)
    return pl.BlockSpec(shape, lambda *_: (0,) * nd, pipeline_mode=pl.Buffered(1))


def _pool_kernel(flags_ref, x_ref, xp_ref, xn_ref, g_ref, pwh_ref, pwl_ref, ps_ref, o_ref, xe_ref):
    i = pl.program_id(0)
    tile = x_ref.shape[0]
    first = flags_ref[0, i]
    last = flags_ref[1, i]
    g = g_ref[...]
    x = x_ref[...]
    u = _rmsnorm(x, g)
    _fill_extended(xe_ref, u, _rmsnorm(xp_ref[...], g), _rmsnorm(xn_ref[...], g), first, last, tile)
    row = lax.broadcasted_iota(jnp.int32, (tile, 1), 0)
    for gi, w in enumerate(POOL_WINDOWS):
        sl = slice(gi * POOL_GROUP, (gi + 1) * POOL_GROUP)
        acc = xe_ref[pl.ds(HALO - w // 2, tile), sl]
        for j in range(-w // 2 + 1, w // 2):
            acc = acc + xe_ref[pl.ds(HALO + j, tile), sl]
        lo_clip = jnp.where(first != 0, jnp.maximum(w // 2 - row, 0), 0)
        hi_clip = jnp.where(last != 0, jnp.maximum(row + w // 2 - tile, 0), 0)
        cnt = (w - lo_clip - hi_clip).astype(F32)
        diff = acc / cnt - u[:, sl]
        y = _dot3(diff, pwh_ref[gi], pwl_ref[gi])
        o_ref[:, sl] = x[:, sl] + y * ps_ref[:, sl]


def _pool_layer(h, flags, norm_g, pool_w, pool_scale):
    n_tok = h.shape[0]
    tile = POOL_TILE
    pwh, pwl = _split(pool_w)
    grid_spec = pltpu.PrefetchScalarGridSpec(
        num_scalar_prefetch=1,
        grid=(n_tok // tile,),
        in_specs=_halo_specs(tile, n_tok) + [
            _const_spec((1, D_MODEL)), _const_spec(pwh.shape), _const_spec(pwl.shape),
            _const_spec((1, D_MODEL)),
        ],
        out_specs=pl.BlockSpec((tile, D_MODEL), lambda i, f: (i, 0)),
        scratch_shapes=[pltpu.VMEM((tile + 2 * HALO, D_MODEL), F32)],
    )
    return pl.pallas_call(
        _pool_kernel,
        grid_spec=grid_spec,
        out_shape=jax.ShapeDtypeStruct((n_tok, D_MODEL), F32),
        compiler_params=_params(("parallel",)),
        name="pool",
    )(flags, h, h, h, norm_g.reshape(1, -1), pwh, pwl, pool_scale.reshape(1, -1))


def _top_values(cur, n, out_ref, slot):
    for r in range(n):
        mx = jnp.max(cur, axis=0, keepdims=True)
        out_ref[slot, r:r + 1, :] = mx
        cur = jnp.where(cur == mx, -jnp.inf, cur)


def _router_kernel(h_ref, g_ref, qwh_ref, qwl_ref, skh_ref, skl_ref,
                   hnT_ref, thr_ref, e1_ref, s2_ref, e2_ref, s_ref, tops_ref, cand_ref):
    hn = _rmsnorm(h_ref[...], g_ref[...])
    hnT_ref[...] = hn.T.astype(BF16)
    q = _dot3(hn, qwh_ref[...], qwl_ref[...])
    for hp in range(2 * PEER_HEADS):
        qs = q[:, hp * PEER_KEYS:(hp + 1) * PEER_KEYS]
        qh, ql = _split(qs)
        skh = skh_ref[hp]
        s_ref[hp] = _dot(skh, qh, NT) + (_dot(skl_ref[hp], qh, NT) + _dot(skh, ql, NT))
    for h in range(PEER_HEADS):
        s1 = s_ref[2 * h]
        s2 = s_ref[2 * h + 1]
        _top_values(s1, NTOP, tops_ref, 0)
        _top_values(s2, NTOP, tops_ref, 1)
        cand_ref[...] = jnp.full(cand_ref.shape, -jnp.inf, F32)
        off = 0
        for i, n in enumerate(CAND_COUNTS):
            cand_ref[0, off:off + n, :] = tops_ref[0, i:i + 1, :] + tops_ref[1, 0:n, :]
            off += n
        m1 = tops_ref[0, 0:1, :]
        m2 = tops_ref[1, 0:1, :]
        _top_values(cand_ref[0], NTOP, tops_ref, 2)
        top = tops_ref[2, 0:PEER_TOPK, :]
        z = jnp.sum(jnp.exp(top - top[0:1, :]), axis=0, keepdims=True)
        tau = 0.5 * (tops_ref[2, PEER_TOPK - 1:PEER_TOPK, :] + tops_ref[2, PEER_TOPK:PEER_TOPK + 1, :])
        thr_ref[h] = tau - s1
        e1_ref[h] = jnp.exp(s1 - m1) * (1.0 / z)
        s2_ref[h] = s2
        e2_ref[h] = jnp.exp(s2 - m2)


def _peer_router(h, norm_g, q_w, subkeys):
    n_tok = h.shape[0]
    tile = ROUTER_TILE
    qwh, qwl = _split(q_w)
    sk = subkeys.reshape(2 * PEER_HEADS, PEER_KEYS, PEER_KEYS)
    skh, skl = _split(sk)
    per_head = pl.BlockSpec((PEER_HEADS, PEER_KEYS, tile), lambda i: (0, 0, i))
    per_head_shape = jax.ShapeDtypeStruct((PEER_HEADS, PEER_KEYS, n_tok), F32)
    return pl.pallas_call(
        _router_kernel,
        grid=(n_tok // tile,),
        in_specs=[
            pl.BlockSpec((tile, D_MODEL), lambda i: (i, 0)),
            _const_spec((1, D_MODEL)),
            _const_spec(qwh.shape), _const_spec(qwl.shape),
            _const_spec(skh.shape), _const_spec(skl.shape),
        ],
        out_specs=[pl.BlockSpec((D_MODEL, tile), lambda i: (0, i))] + [per_head] * 4,
        out_shape=[jax.ShapeDtypeStruct((D_MODEL, n_tok), BF16)] + [per_head_shape] * 4,
        scratch_shapes=[pltpu.VMEM((2 * PEER_HEADS, PEER_KEYS, tile), F32),
                        pltpu.VMEM((3, 24, tile), F32), pltpu.VMEM((1, CAND_ROWS, tile), F32)],
        compiler_params=_params(("parallel",)),
        name="peer_router",
    )(h, norm_g.reshape(1, -1), qwh, qwl, skh, skl)


def _dense_kernel(h_ref, hnT_ref, thr_ref, e1_ref, s2_ref, e2_ref, u_ref, vT_ref, gf_ref, o_ref,
                  acc_ref, *, final_norm):
    j = pl.program_id(1)
    eblk, tile = u_ref.shape[0], hnT_ref.shape[1]

    @pl.when(j == 0)
    def _():
        acc_ref[...] = jnp.zeros(acc_ref.shape, F32)

    hnT = hnT_ref[...]
    n_sub = eblk // DENSE_SUB
    act_next = _dot(u_ref[0:DENSE_SUB, :], hnT)
    for sb in range(n_sub):
        r0 = sb * DENSE_SUB
        act = act_next
        if sb + 1 < n_sub:
            act_next = _dot(u_ref[r0 + DENSE_SUB:r0 + 2 * DENSE_SUB, :], hnT)
        p_rows = []
        for al in range(r0 // PEER_KEYS, (r0 + DENSE_SUB) // PEER_KEYS):
            p_cols = []
            for tc in range(tile // 128):
                cols = slice(tc * 128, (tc + 1) * 128)
                gate = jnp.zeros((PEER_KEYS, 128), F32)
                for h in range(PEER_HEADS):
                    thr_row = thr_ref[h, al:al + 1, cols]
                    e1_row = e1_ref[h, al:al + 1, cols]
                    gate = gate + jnp.where(s2_ref[h, :, cols] >= thr_row, e2_ref[h, :, cols] * e1_row, 0.0)
                x = act[al * PEER_KEYS - r0:(al + 1) * PEER_KEYS - r0, cols]
                gelu = 0.5 * x * (1.0 + lax.erf(x * INV_SQRT2))
                p_cols.append((gate * gelu).astype(BF16))
            p_rows.append(jnp.concatenate(p_cols, axis=1))
        p = jnp.concatenate(p_rows, axis=0)
        acc_ref[...] += _dot(vT_ref[:, r0:r0 + DENSE_SUB], p)

    @pl.when(j == pl.num_programs(1) - 1)
    def _():
        out = h_ref[...] + acc_ref[...].T
        if final_norm:
            out = _rmsnorm(out, gf_ref[...])
        o_ref[...] = out


def _peer_dense(h, hnT, thr, e1, s2, e2, u_bf, vT_bf, final_g, final_norm):
    n_tok = h.shape[0]
    tile, eblk = DENSE_TILE, DENSE_EBLK
    rows = pl.BlockSpec((PEER_HEADS, eblk // PEER_KEYS, tile), lambda i, j: (0, j, i))
    full = pl.BlockSpec((PEER_HEADS, PEER_KEYS, tile), lambda i, j: (0, 0, i))
    return pl.pallas_call(
        functools.partial(_dense_kernel, final_norm=final_norm),
        grid=(n_tok // tile, PEER_EXPERTS // eblk),
        in_specs=[
            pl.BlockSpec((tile, D_MODEL), lambda i, j: (i, 0)),
            pl.BlockSpec((D_MODEL, tile), lambda i, j: (0, i)),
            rows, rows, full, full,
            pl.BlockSpec((eblk, D_MODEL), lambda i, j: (j, 0)),
            pl.BlockSpec((D_MODEL, eblk), lambda i, j: (0, j)),
            _const_spec((1, D_MODEL)),
        ],
        out_specs=pl.BlockSpec((tile, D_MODEL), lambda i, j: (i, 0)),
        out_shape=jax.ShapeDtypeStruct((n_tok, D_MODEL), F32),
        scratch_shapes=[
            pltpu.VMEM((D_MODEL, tile), F32),
        ],
        compiler_params=_params(("parallel", "arbitrary")),
        name="peer_dense",
    )(h, hnT, thr, e1, s2, e2, u_bf, vT_bf, final_g.reshape(1, -1))


def _peer_layer(h, norm_g, q_w, subkeys, u_tab, v_tab, final_g, final_norm):
    hnT, thr, e1, s2, e2 = _peer_router(h, norm_g, q_w, subkeys)
    return _peer_dense(h, hnT, thr, e1, s2, e2, u_tab.astype(BF16), v_tab.T.astype(BF16), final_g, final_norm)


def _head_sum(x, ind_ref, indT_ref):
    return _dot2(_dot2(x, ind_ref[...]), indT_ref[...])


def _proj_kernel(flags_ref, x_ref, xp_ref, xn_ref, g_ref, mu_ref, wh_ref, wl_ref,
                 dw0_ref, dw1_ref, dw2_ref, a0_ref, a1_ref, a2_ref, g1_ref, g2_ref,
                 kk_ref_w, ka_ref, ind_ref, indT_ref,
                 r_ref, k_ref, v_ref, gate_ref, kko_ref, lw_ref, kd_ref, beta_ref, xe_ref):
    i = pl.program_id(0)
    tile = x_ref.shape[0]
    first = flags_ref[0, i]
    last = flags_ref[1, i]
    g = g_ref[...]
    u = _rmsnorm(x_ref[...], g)
    _fill_extended(xe_ref, u, _rmsnorm(xp_ref[...], g), _rmsnorm(xn_ref[...], g), first, last, tile)
    xx = 0.5 * (xe_ref[pl.ds(HALO - 1, tile), :] + xe_ref[pl.ds(HALO + 1, tile), :]) - u
    xr, xw, xk, xv, xa, xg = (u + xx * mu_ref[m:m + 1, :] for m in range(6))
    r = _dot3(xr, wh_ref[0], wl_ref[0])
    k = _dot3(xk, wh_ref[1], wl_ref[1])
    v = _dot3(xv, wh_ref[2], wl_ref[2])
    r_ref[...] = r
    k_ref[...] = k
    v_ref[...] = v
    gate_ref[...] = _dot3f(jax.nn.sigmoid(_dot3f(xg, g1_ref[...])), g2_ref[...])
    kk = k * kk_ref_w[...]
    nrm = jnp.sqrt(_head_sum(kk * kk, ind_ref, indT_ref))
    kk = kk / jnp.maximum(nrm, 1e-12)
    kko_ref[...] = kk
    for d in range(2):
        z = dw0_ref[d:d + 1, :] + _dot3f(jnp.tanh(_dot3f(xw, dw1_ref[d])), dw2_ref[d])
        lw_ref[d] = -DECAY_SCALE * jax.nn.sigmoid(z)
        a = jax.nn.sigmoid(a0_ref[d:d + 1, :] + _dot3f(_dot3f(xa, a1_ref[d]), a2_ref[d]))
        kd_ref[d] = k * (1.0 + (a - 1.0) * ka_ref[...])
        beta_ref[d] = kk * a


def _head_indicator():
    ind = (np.arange(D_MODEL)[:, None] // RWKV_HEAD == np.arange(RWKV_HEADS)[None, :]).astype(np.float32)
    pad = np.zeros((D_MODEL, 128), np.float32)
    pad[:, :RWKV_HEADS] = ind
    return jnp.asarray(pad, BF16), jnp.asarray(pad.T, BF16)


def _rwkv_proj(h, flags, norm_g, mu, w_rkv, dec_w0, dec_w1, dec_w2, icl_a0, icl_a1, icl_a2,
               gate_g1, gate_g2, k_k, k_a):
    n_tok = h.shape[0]
    tile = PROJ_TILE
    wh, wl = _split(w_rkv)
    ind, indT = _head_indicator()
    consts = [norm_g.reshape(1, -1), mu, wh, wl, dec_w0, dec_w1, dec_w2, icl_a0, icl_a1, icl_a2,
              gate_g1, gate_g2, k_k.reshape(1, -1), k_a.reshape(1, -1), ind, indT]

    tok = pl.BlockSpec((tile, D_MODEL), lambda i, f: (i, 0))
    tok2 = pl.BlockSpec((2, tile, D_MODEL), lambda i, f: (0, i, 0))
    one = jax.ShapeDtypeStruct((n_tok, D_MODEL), F32)
    two = jax.ShapeDtypeStruct((2, n_tok, D_MODEL), F32)
    grid_spec = pltpu.PrefetchScalarGridSpec(
        num_scalar_prefetch=1,
        grid=(n_tok // tile,),
        in_specs=_halo_specs(tile, n_tok) + [_const_spec(a.shape) for a in consts],
        out_specs=[tok] * 5 + [tok2] * 3,
        scratch_shapes=[pltpu.VMEM((tile + 2 * HALO, D_MODEL), F32)],
    )
    return pl.pallas_call(
        _proj_kernel,
        grid_spec=grid_spec,
        out_shape=[one] * 5 + [two] * 3,
        compiler_params=_params(("parallel",)),
        name="rwkv_proj",
    )(flags, h, h, h, *consts)


def _scan_prep(rev, r_ref, lw_ref, kd_ref, kk_ref, beta_ref, incl):
    L = r_ref.shape[0]
    lw = lw_ref[...]
    lw_hi = lw.astype(BF16)
    lw_r = lw - lw_hi.astype(F32)
    lw_mid = lw_r.astype(BF16)
    lw_lo = (lw_r - lw_mid.astype(F32)).astype(BF16)
    inc_bf = incl.astype(BF16)
    cs = _dot(inc_bf, lw_hi) + (_dot(inc_bf, lw_mid) + _dot(inc_bf, lw_lo))
    end = 0 if rev else L - 1
    cs_end = cs[end:end + 1, :]
    g_inv = jnp.exp(-cs)
    g_rest = jnp.exp(cs_end - cs)
    beta = beta_ref[...]
    kd = kd_ref[...]
    return dict(
        alpha_b=-kk_ref[...] * jnp.exp(cs - lw), r_b=r_ref[...] * jnp.exp(cs),
        beta_t=beta * g_inv, k_t=kd * g_inv, beta_h=beta * g_rest, k_h=kd * g_rest,
        g_end=jnp.exp(cs_end))


def _scan_kernel(flags_ref, rf, lwf, kdf, vf, kkf, bf, rb, lwb, kdb, vb, kkb, bb, yf_ref, yb_ref, stf_ref, stb_ref):
    c = pl.program_id(0)
    n = pl.num_programs(0)
    L = rf.shape[0]
    N = RWKV_HEAD
    row = lax.broadcasted_iota(jnp.int32, (L, L), 0)
    col = lax.broadcasted_iota(jnp.int32, (L, L), 1)
    eye = (row == col).astype(F32)
    incl = (row >= col, row <= col)
    strict = (row > col, row < col)
    prep = (_scan_prep(False, rf, lwf, kdf, kkf, bf, incl[0]), _scan_prep(True, rb, lwb, kdb, kkb, bb, incl[1]))
    v_all = (vf[...], vb[...])
    reset = (flags_ref[0, c], flags_ref[1, n - 1 - c])
    st_refs = (stf_ref, stb_ref)
    y_refs = (yf_ref, yb_ref)
    chains = [(d, h) for h in range(RWKV_HEADS) for d in range(2)]

    def cols(d, h, name):
        return prep[d][name][:, h * N:(h + 1) * N]

    t0 = [jnp.where(reset[d] != 0, 0.0, st_refs[d][h]) for d, h in chains]
    x2 = [jnp.concatenate([cols(d, h, "alpha_b"), cols(d, h, "r_b")], axis=0) for d, h in chains]
    y2 = [jnp.concatenate([cols(d, h, "beta_t"), cols(d, h, "k_t")], axis=0) for d, h in chains]
    aa = [_dot3f(a, b, NT) for a, b in zip(x2, y2)]
    xt = [_dot1f(a, b) for a, b in zip(x2, t0)]
    a_ab = [jnp.where(strict[d], m[0:L, 0:L], 0.0) for (d, h), m in zip(chains, aa)]
    a_ak = [jnp.where(strict[d], m[0:L, L:2 * L], 0.0) for (d, h), m in zip(chains, aa)]
    a_r = [jnp.concatenate([jnp.where(incl[d], m[L:2 * L, 0:L], 0.0), jnp.where(incl[d], m[L:2 * L, L:2 * L], 0.0)],
                           axis=1) for (d, h), m in zip(chains, aa)]
    vh = [v_all[d][:, h * N:(h + 1) * N] for d, h in chains]
    rhs = [x[0:L] + _dot1f(a, b) for x, a, b in zip(xt, a_ak, vh)]
    minv = [eye + a for a in a_ab]
    ap = a_ab
    for _ in range(int(math.log2(L)) - 1):
        ap = [_dot1f(a, a) for a in ap]
        minv = [m + _dot1f(m, a) for m, a in zip(minv, ap)]
    u = [_dot1f(m, b) for m, b in zip(minv, rhs)]
    uv = [jnp.concatenate([a, b], axis=0) for a, b in zip(u, vh)]
    y = [x[L:2 * L] + _dot3f(a, b) for x, a, b in zip(xt, a_r, uv)]
    bk = [jnp.concatenate([cols(d, h, "beta_h"), cols(d, h, "k_h")], axis=0) for d, h in chains]
    t1 = [_dot1f(a, b, TN) for a, b in zip(bk, uv)]
    for i, (d, h) in enumerate(chains):
        y_refs[d][:, h * N:(h + 1) * N] = y[i]
        st_refs[d][h] = t0[i] * prep[d]["g_end"][:, h * N:(h + 1) * N].T + t1[i]


def _rwkv_scan(flags, r, lw, kd, v, kk, beta):
    n_tok = r.shape[0]
    L = SCAN_CHUNK
    n = n_tok // L
    fwd = pl.BlockSpec((L, D_MODEL), lambda c, f: (c, 0))
    bwd = pl.BlockSpec((L, D_MODEL), lambda c, f: (n - 1 - c, 0))

    def dspec(d, rev):
        return pl.BlockSpec((None, L, D_MODEL), (lambda c, f: (d, n - 1 - c, 0)) if rev else (lambda c, f: (d, c, 0)))

    grid_spec = pltpu.PrefetchScalarGridSpec(
        num_scalar_prefetch=1,
        grid=(n,),
        in_specs=[fwd, dspec(0, False), dspec(0, False), fwd, fwd, dspec(0, False),
                  bwd, dspec(1, True), dspec(1, True), bwd, bwd, dspec(1, True)],
        out_specs=[fwd, bwd],
        scratch_shapes=[pltpu.VMEM((RWKV_HEADS, RWKV_HEAD, RWKV_HEAD), F32)] * 2,
    )
    one = jax.ShapeDtypeStruct((n_tok, D_MODEL), F32)
    return pl.pallas_call(
        _scan_kernel,
        grid_spec=grid_spec,
        out_shape=[one, one],
        compiler_params=_params(("arbitrary",)),
        name="rwkv_scan",
    )(flags, r, lw, kd, v, kk, beta, r, lw, kd, v, kk, beta)


def _post_kernel(h_ref, yf_ref, yb_ref, r_ref, k_ref, v_ref, g_ref, rk_ref, lg_ref, lb_ref,
                 woh_ref, wol_ref, ind_ref, indT_ref, o_ref):
    y = yf_ref[...] + yb_ref[...]
    inv_n = 1.0 / RWKV_HEAD
    mean = _head_sum(y, ind_ref, indT_ref) * inv_n
    yc = y - mean
    var = _head_sum(yc * yc, ind_ref, indT_ref) * inv_n
    yn = yc * lax.rsqrt(var + LNX_EPS) * lg_ref[...] + lb_ref[...]
    bonus = _head_sum(r_ref[...] * k_ref[...] * rk_ref[...], ind_ref, indT_ref) * v_ref[...]
    out = _dot3((yn + bonus) * g_ref[...], woh_ref[...], wol_ref[...])
    o_ref[...] = h_ref[...] + out


def _rwkv_post(h, yf, yb, r, k, v, g, r_k, lnx_g, lnx_b, w_o):
    n_tok = h.shape[0]
    tile = POST_TILE
    woh, wol = _split(w_o)
    ind, indT = _head_indicator()
    tok = pl.BlockSpec((tile, D_MODEL), lambda i: (i, 0))
    consts = [r_k.reshape(1, -1), lnx_g.reshape(1, -1), lnx_b.reshape(1, -1), woh, wol, ind, indT]
    return pl.pallas_call(
        _post_kernel,
        grid=(n_tok // tile,),
        in_specs=[tok] * 7 + [_const_spec(a.shape) for a in consts],
        out_specs=tok,
        out_shape=jax.ShapeDtypeStruct((n_tok, D_MODEL), F32),
        compiler_params=_params(("parallel",)),
        name="rwkv_post",
    )(h, yf, yb, r, k, v, g, *consts)


def _rwkv_layer(h, seq_lens, norm_g, mu, w_rkv, w_o, dec_w0, dec_w1, dec_w2, icl_a0, icl_a1, icl_a2,
                gate_g1, gate_g2, k_k, k_a, r_k, lnx_g, lnx_b):
    r, k, v, g, kk, lw, kd, beta = _rwkv_proj(
        h, _tile_flags(seq_lens, PROJ_TILE), norm_g, mu, w_rkv, dec_w0, dec_w1, dec_w2,
        icl_a0, icl_a1, icl_a2, gate_g1, gate_g2, k_k, k_a)
    yf, yb = _rwkv_scan(_tile_flags(seq_lens, SCAN_CHUNK), r, lw, kd, v, kk, beta)
    return _rwkv_post(h, yf, yb, r, k, v, g, r_k, lnx_g, lnx_b, w_o)


def _trunk(h, seq_lens, norm_mix, norm_ffn, norm_final, pool_w, pool_scale,
           rwkv_mu, rwkv_w_rkv, rwkv_w_o, rwkv_dec_w0, rwkv_dec_w1, rwkv_dec_w2,
           rwkv_icl_a0, rwkv_icl_a1, rwkv_icl_a2, rwkv_gate_g1, rwkv_gate_g2,
           rwkv_k_k, rwkv_k_a, rwkv_r_k, rwkv_lnx_g, rwkv_lnx_b,
           peer_q, peer_subkeys, peer_u, peer_v):
    depth = norm_mix.shape[0]
    for i in range(depth):
        j = i // 2
        if i % 2 == 0:
            h = _pool_layer(h, _tile_flags(seq_lens, POOL_TILE), norm_mix[i], pool_w[j], pool_scale[j])
        else:
            h = _rwkv_layer(h, seq_lens, norm_mix[i], rwkv_mu[j], rwkv_w_rkv[j], rwkv_w_o[j],
                            rwkv_dec_w0[j], rwkv_dec_w1[j], rwkv_dec_w2[j], rwkv_icl_a0[j],
                            rwkv_icl_a1[j], rwkv_icl_a2[j], rwkv_gate_g1[j], rwkv_gate_g2[j],
                            rwkv_k_k[j], rwkv_k_a[j], rwkv_r_k[j].reshape(-1), rwkv_lnx_g[j], rwkv_lnx_b[j])
        h = _peer_layer(h, norm_ffn[i], peer_q[i], peer_subkeys[i], peer_u[i], peer_v[i],
                        norm_final, final_norm=(i == depth - 1))
    return h


def kernel(x_prompt, x_sample, norm_mix, norm_ffn, norm_final, pool_w, pool_scale, rwkv_mu, rwkv_w_rkv, rwkv_w_o, rwkv_dec_w0, rwkv_dec_w1, rwkv_dec_w2, rwkv_icl_a0, rwkv_icl_a1, rwkv_icl_a2, rwkv_gate_g1, rwkv_gate_g2, rwkv_k_k, rwkv_k_a, rwkv_r_k, rwkv_lnx_g, rwkv_lnx_b, peer_q, peer_subkeys, peer_u, peer_v):
    bp, sp, d = x_prompt.shape
    bs, ss, _ = x_sample.shape
    seq_lens = (sp,) * bp + (ss,) * bs
    h = jnp.concatenate([x_prompt.reshape(-1, d), x_sample.reshape(-1, d)], axis=0)
    y = _trunk(h, seq_lens, norm_mix, norm_ffn, norm_final, pool_w, pool_scale,
               rwkv_mu, rwkv_w_rkv, rwkv_w_o, rwkv_dec_w0, rwkv_dec_w1, rwkv_dec_w2,
               rwkv_icl_a0, rwkv_icl_a1, rwkv_icl_a2, rwkv_gate_g1, rwkv_gate_g2,
               rwkv_k_k, rwkv_k_a, rwkv_r_k, rwkv_lnx_g, rwkv_lnx_b,
               peer_q, peer_subkeys, peer_u, peer_v)
    n_p = bp * sp
    return (y[:n_p].reshape(bp, sp, d), y[n_p:].reshape(bs, ss, d))
```

```python
import functools
import math

import numpy as np
import jax
import jax.numpy as jnp
from jax import lax
from jax.experimental import pallas as pl
from jax.experimental.pallas import tpu as pltpu

F32 = jnp.float32
BF16 = jnp.bfloat16

D_MODEL = 1024
NORM_EPS = 1e-6
POOL_WINDOWS = (2, 4, 8, 16)
POOL_GROUP = D_MODEL // len(POOL_WINDOWS)
RWKV_HEAD = 64
RWKV_HEADS = D_MODEL // RWKV_HEAD
DECAY_SCALE = math.exp(-0.5)
LNX_EPS = 64e-5
PEER_HEADS = 8
PEER_KEYS = 128
PEER_EXPERTS = PEER_KEYS * PEER_KEYS
PEER_TOPK = 16
INV_SQRT2 = 0.7071067811865476

HALO = 8
VMEM_LIMIT = 56 * 1024 * 1024

POOL_TILE = 512
PROJ_TILE = 256
POST_TILE = 256
ROUTER_TILE = 256
DENSE_TILE = 512
DENSE_EBLK = 1024
DENSE_SUB = 256
SCAN_CHUNK = 64
NTOP = PEER_TOPK + 1
CAND_COUNTS = tuple(min(NTOP, NTOP // (i + 1)) for i in range(NTOP))
CAND_ROWS = -(-sum(CAND_COUNTS) // 8) * 8


def _params(sem):
    return pltpu.CompilerParams(dimension_semantics=sem, vmem_limit_bytes=VMEM_LIMIT)


def _split(w):
    hi = w.astype(BF16)
    lo = (w - hi.astype(F32)).astype(BF16)
    return hi, lo


def _dot(a, b, dims=(((1,), (0,)), ((), ()))):
    return lax.dot_general(a, b, dims, preferred_element_type=F32)


def _dot3(a, b_hi, b_lo, dims=(((1,), (0,)), ((), ()))):
    a_hi, a_lo = _split(a)
    return _dot(a_hi, b_hi, dims) + (_dot(a_lo, b_hi, dims) + _dot(a_hi, b_lo, dims))


def _dot2(a, b_exact, dims=(((1,), (0,)), ((), ()))):
    a_hi, a_lo = _split(a)
    return _dot(a_hi, b_exact, dims) + _dot(a_lo, b_exact, dims)


def _dot3f(a, b, dims=(((1,), (0,)), ((), ()))):
    b_hi, b_lo = _split(b)
    return _dot3(a, b_hi, b_lo, dims)


def _dot1f(a, b, dims=(((1,), (0,)), ((), ()))):
    return _dot(a.astype(BF16), b.astype(BF16), dims)


NT = (((1,), (1,)), ((), ()))
TN = (((0,), (0,)), ((), ()))


def _rmsnorm(x, g):
    return x * lax.rsqrt(jnp.mean(x * x, axis=-1, keepdims=True) + NORM_EPS) * g


def _tile_flags(seq_lens, tile):
    first, last = [], []
    for s in seq_lens:
        n = s // tile
        assert n * tile == s
        first += [1] + [0] * (n - 1)
        last += [0] * (n - 1) + [1]
    return jnp.asarray(np.array([first, last], np.int32))


def _halo_specs(tile, n_tok):
    r = tile // HALO
    nb = n_tok // HALO
    return [
        pl.BlockSpec((tile, D_MODEL), lambda i, f: (i, 0)),
        pl.BlockSpec((HALO, D_MODEL), lambda i, f: (jnp.maximum(i * r - 1, 0), 0)),
        pl.BlockSpec((HALO, D_MODEL), lambda i, f: (jnp.minimum((i + 1) * r, nb - 1), 0)),
    ]


def _fill_extended(xe_ref, u, up, un, first, last, tile):
    xe_ref[0:HALO, :] = jnp.where(first != 0, 0.0, up)
    xe_ref[HALO:HALO + tile, :] = u
    xe_ref[HALO + tile:2 * HALO + tile, :] = jnp.where(last != 0, 0.0, un)


def _const_spec(shape):
    nd = len(shape)
    return pl.BlockSpec(shape, lambda *_: (0,) * nd, pipeline_mode=pl.Buffered(1))


def _pool_kernel(flags_ref, x_ref, xp_ref, xn_ref, g_ref, pwh_ref, pwl_ref, ps_ref, o_ref, xe_ref):
    i = pl.program_id(0)
    tile = x_ref.shape[0]
    first = flags_ref[0, i]
    last = flags_ref[1, i]
    g = g_ref[...]
    x = x_ref[...]
    u = _rmsnorm(x, g)
    _fill_extended(xe_ref, u, _rmsnorm(xp_ref[...], g), _rmsnorm(xn_ref[...], g), first, last, tile)
    row = lax.broadcasted_iota(jnp.int32, (tile, 1), 0)
    for gi, w in enumerate(POOL_WINDOWS):
        sl = slice(gi * POOL_GROUP, (gi + 1) * POOL_GROUP)
        acc = xe_ref[pl.ds(HALO - w // 2, tile), sl]
        for j in range(-w // 2 + 1, w // 2):
            acc = acc + xe_ref[pl.ds(HALO + j, tile), sl]
        lo_clip = jnp.where(first != 0, jnp.maximum(w // 2 - row, 0), 0)
        hi_clip = jnp.where(last != 0, jnp.maximum(row + w // 2 - tile, 0), 0)
        cnt = (w - lo_clip - hi_clip).astype(F32)
        diff = acc / cnt - u[:, sl]
        y = _dot3(diff, pwh_ref[gi], pwl_ref[gi])
        o_ref[:, sl] = x[:, sl] + y * ps_ref[:, sl]


def _pool_layer(h, flags, norm_g, pool_w, pool_scale):
    n_tok = h.shape[0]
    tile = POOL_TILE
    pwh, pwl = _split(pool_w)
    grid_spec = pltpu.PrefetchScalarGridSpec(
        num_scalar_prefetch=1,
        grid=(n_tok // tile,),
        in_specs=_halo_specs(tile, n_tok) + [
            _const_spec((1, D_MODEL)), _const_spec(pwh.shape), _const_spec(pwl.shape),
            _const_spec((1, D_MODEL)),
        ],
        out_specs=pl.BlockSpec((tile, D_MODEL), lambda i, f: (i, 0)),
        scratch_shapes=[pltpu.VMEM((tile + 2 * HALO, D_MODEL), F32)],
    )
    return pl.pallas_call(
        _pool_kernel,
        grid_spec=grid_spec,
        out_shape=jax.ShapeDtypeStruct((n_tok, D_MODEL), F32),
        compiler_params=_params(("parallel",)),
        name="pool",
    )(flags, h, h, h, norm_g.reshape(1, -1), pwh, pwl, pool_scale.reshape(1, -1))


def _top_values(cur, n, out_ref, slot):
    for r in range(n):
        mx = jnp.max(cur, axis=0, keepdims=True)
        out_ref[slot, r:r + 1, :] = mx
        cur = jnp.where(cur == mx, -jnp.inf, cur)


def _bf16_pair_words(x):
    bits = pltpu.bitcast(x.astype(BF16).astype(F32), jnp.uint32)
    return bits | lax.shift_right_logical(bits, jnp.uint32(16))


def _router_kernel(h_ref, g_ref, qwh_ref, qwl_ref, skh_ref, skl_ref,
                   hnT_ref, cnt_ref, e1_ref, rank_ref, e2_ref, s_ref, tops_ref, cand_ref):
    hn = _rmsnorm(h_ref[...], g_ref[...])
    hnT_ref[...] = hn.T.astype(BF16)
    q = _dot3(hn, qwh_ref[...], qwl_ref[...])
    for hp in range(2 * PEER_HEADS):
        qs = q[:, hp * PEER_KEYS:(hp + 1) * PEER_KEYS]
        qh, ql = _split(qs)
        skh = skh_ref[hp]
        s_ref[hp] = _dot(skh, qh, NT) + (_dot(skl_ref[hp], qh, NT) + _dot(skh, ql, NT))
    for h in range(PEER_HEADS):
        s1 = s_ref[2 * h]
        s2 = s_ref[2 * h + 1]
        _top_values(s1, NTOP, tops_ref, 0)
        _top_values(s2, NTOP, tops_ref, 1)
        cand_ref[...] = jnp.full(cand_ref.shape, -jnp.inf, F32)
        off = 0
        for i, n in enumerate(CAND_COUNTS):
            cand_ref[0, off:off + n, :] = tops_ref[0, i:i + 1, :] + tops_ref[1, 0:n, :]
            off += n
        m1 = tops_ref[0, 0:1, :]
        m2 = tops_ref[1, 0:1, :]
        _top_values(cand_ref[0], NTOP, tops_ref, 2)
        top = tops_ref[2, 0:PEER_TOPK, :]
        z = jnp.sum(jnp.exp(top - top[0:1, :]), axis=0, keepdims=True)
        tau = 0.5 * (tops_ref[2, PEER_TOPK - 1:PEER_TOPK, :] + tops_ref[2, PEER_TOPK:PEER_TOPK + 1, :])
        thr = tau - s1
        cnt = jnp.zeros_like(s1)
        rank = jnp.zeros_like(s2)
        for k in range(NTOP):
            tk = tops_ref[1, k:k + 1, :]
            cnt = cnt + jnp.where(tk >= thr, 1.0, 0.0)
            rank = rank + jnp.where(tk > s2, 1.0, 0.0)
        cnt_ref[h] = _bf16_pair_words(cnt)
        e1_ref[h] = _bf16_pair_words(jnp.exp(s1 - m1) * (1.0 / z))
        e2 = jnp.exp(s2 - m2)
        for tc in range(rank_ref.shape[1]):
            cols = slice(tc * 128, (tc + 1) * 128)
            rank_ref[h, tc] = pltpu.bitcast(rank[:, cols].astype(BF16), jnp.uint32)
            e2_ref[h, tc] = pltpu.bitcast(e2[:, cols].astype(BF16), jnp.uint32)


def _peer_router(h, norm_g, q_w, subkeys):
    n_tok = h.shape[0]
    tile = ROUTER_TILE
    qwh, qwl = _split(q_w)
    sk = subkeys.reshape(2 * PEER_HEADS, PEER_KEYS, PEER_KEYS)
    skh, skl = _split(sk)
    rows = pl.BlockSpec((PEER_HEADS, PEER_KEYS, tile), lambda i: (0, 0, i))
    rows_shape = jax.ShapeDtypeStruct((PEER_HEADS, PEER_KEYS, n_tok), jnp.uint32)
    tiles = pl.BlockSpec((PEER_HEADS, tile // 128, PEER_KEYS // 2, 128), lambda i: (0, i, 0, 0))
    tiles_shape = jax.ShapeDtypeStruct((PEER_HEADS, n_tok // 128, PEER_KEYS // 2, 128), jnp.uint32)
    return pl.pallas_call(
        _router_kernel,
        grid=(n_tok // tile,),
        in_specs=[
            pl.BlockSpec((tile, D_MODEL), lambda i: (i, 0)),
            _const_spec((1, D_MODEL)),
            _const_spec(qwh.shape), _const_spec(qwl.shape),
            _const_spec(skh.shape), _const_spec(skl.shape),
        ],
        out_specs=[pl.BlockSpec((D_MODEL, tile), lambda i: (0, i)), rows, rows, tiles, tiles],
        out_shape=[jax.ShapeDtypeStruct((D_MODEL, n_tok), BF16), rows_shape, rows_shape, tiles_shape, tiles_shape],
        scratch_shapes=[pltpu.VMEM((2 * PEER_HEADS, PEER_KEYS, tile), F32),
                        pltpu.VMEM((3, 24, tile), F32), pltpu.VMEM((1, CAND_ROWS, tile), F32)],
        compiler_params=_params(("parallel",)),
        name="peer_router",
    )(h, norm_g.reshape(1, -1), qwh, qwl, skh, skl)


def _rows_as_bf16(words):
    return pltpu.bitcast(jnp.broadcast_to(words, (PEER_KEYS // 2, words.shape[1])), BF16)


def _dense_kernel(h_ref, hnT_ref, cnt_ref, e1_ref, rank_ref, e2_ref, u_ref, vT_ref, gf_ref, o_ref,
                  acc_ref, act_ref, p_ref, *, final_norm):
    j = pl.program_id(1)
    eblk, tile = u_ref.shape[0], hnT_ref.shape[1]

    @pl.when(j == 0)
    def _():
        acc_ref[...] = jnp.zeros(acc_ref.shape, F32)

    hnT = hnT_ref[...]
    n_sub = eblk // DENSE_SUB
    act_ref[0] = _dot(u_ref[0:DENSE_SUB, :], hnT)
    for sb in range(n_sub):
        r0 = sb * DENSE_SUB
        if sb + 1 < n_sub:
            act_ref[(sb + 1) % 2] = _dot(u_ref[r0 + DENSE_SUB:r0 + 2 * DENSE_SUB, :], hnT)
        als = range(r0 // PEER_KEYS, (r0 + DENSE_SUB) // PEER_KEYS)
        for tc in range(tile // 128):
            cols = slice(tc * 128, (tc + 1) * 128)
            gates = [jnp.zeros((PEER_KEYS, 128), BF16) for _ in als]
            for h in range(PEER_HEADS):
                rank = pltpu.bitcast(rank_ref[h, tc], BF16)
                e2 = pltpu.bitcast(e2_ref[h, tc], BF16)
                for i, al in enumerate(als):
                    cnt_row = _rows_as_bf16(cnt_ref[h, al:al + 1, cols])
                    e1_row = _rows_as_bf16(e1_ref[h, al:al + 1, cols])
                    gates[i] = gates[i] + jnp.where(rank < cnt_row, e2 * e1_row, jnp.zeros((), BF16))
            for i, al in enumerate(als):
                rows = slice(al * PEER_KEYS - r0, (al + 1) * PEER_KEYS - r0)
                x = act_ref[sb % 2, rows, cols]
                gelu = 0.5 * x * (1.0 + lax.erf(x * INV_SQRT2))
                p_ref[sb % 2, rows, cols] = gates[i] * gelu.astype(BF16)
        acc_ref[...] += _dot(vT_ref[:, r0:r0 + DENSE_SUB], p_ref[sb % 2])

    @pl.when(j == pl.num_programs(1) - 1)
    def _():
        out = h_ref[...] + acc_ref[...].T
        if final_norm:
            out = _rmsnorm(out, gf_ref[...])
        o_ref[...] = out


def _peer_dense(h, hnT, cnt, e1, rank, e2, u_bf, vT_bf, final_g, final_norm):
    n_tok = h.shape[0]
    tile, eblk = DENSE_TILE, DENSE_EBLK
    rows = pl.BlockSpec((PEER_HEADS, eblk // PEER_KEYS, tile), lambda i, j: (0, j, i))
    full = pl.BlockSpec((PEER_HEADS, tile // 128, PEER_KEYS // 2, 128), lambda i, j: (0, i, 0, 0))
    return pl.pallas_call(
        functools.partial(_dense_kernel, final_norm=final_norm),
        grid=(n_tok // tile, PEER_EXPERTS // eblk),
        in_specs=[
            pl.BlockSpec((tile, D_MODEL), lambda i, j: (i, 0)),
            pl.BlockSpec((D_MODEL, tile), lambda i, j: (0, i)),
            rows, rows, full, full,
            pl.BlockSpec((eblk, D_MODEL), lambda i, j: (j, 0)),
            pl.BlockSpec((D_MODEL, eblk), lambda i, j: (0, j)),
            _const_spec((1, D_MODEL)),
        ],
        out_specs=pl.BlockSpec((tile, D_MODEL), lambda i, j: (i, 0)),
        out_shape=jax.ShapeDtypeStruct((n_tok, D_MODEL), F32),
        scratch_shapes=[
            pltpu.VMEM((D_MODEL, tile), F32),
            pltpu.VMEM((2, DENSE_SUB, tile), F32),
            pltpu.VMEM((2, DENSE_SUB, tile), BF16),
        ],
        compiler_params=_params(("parallel", "arbitrary")),
        name="peer_dense",
    )(h, hnT, cnt, e1, rank, e2, u_bf, vT_bf, final_g.reshape(1, -1))


def _peer_layer(h, norm_g, q_w, subkeys, u_tab, v_tab, final_g, final_norm):
    hnT, cnt, e1, rank, e2 = _peer_router(h, norm_g, q_w, subkeys)
    return _peer_dense(h, hnT, cnt, e1, rank, e2, u_tab.astype(BF16), v_tab.T.astype(BF16), final_g, final_norm)


def _head_sum(x, ind_ref, indT_ref):
    return _dot2(_dot2(x, ind_ref[...]), indT_ref[...])


def _proj_kernel(flags_ref, x_ref, xp_ref, xn_ref, g_ref, mu_ref, wh_ref, wl_ref,
                 dw0_ref, dw1_ref, dw2_ref, a0_ref, a1_ref, a2_ref, g1_ref, g2_ref,
                 kk_ref_w, ka_ref, ind_ref, indT_ref,
                 r_ref, k_ref, v_ref, gate_ref, kko_ref, lw_ref, kd_ref, beta_ref, xe_ref):
    i = pl.program_id(0)
    tile = x_ref.shape[0]
    first = flags_ref[0, i]
    last = flags_ref[1, i]
    g = g_ref[...]
    u = _rmsnorm(x_ref[...], g)
    _fill_extended(xe_ref, u, _rmsnorm(xp_ref[...], g), _rmsnorm(xn_ref[...], g), first, last, tile)
    xx = 0.5 * (xe_ref[pl.ds(HALO - 1, tile), :] + xe_ref[pl.ds(HALO + 1, tile), :]) - u
    xr, xw, xk, xv, xa, xg = (u + xx * mu_ref[m:m + 1, :] for m in range(6))
    r = _dot3(xr, wh_ref[0], wl_ref[0])
    k = _dot3(xk, wh_ref[1], wl_ref[1])
    v = _dot3(xv, wh_ref[2], wl_ref[2])
    r_ref[...] = r
    k_ref[...] = k
    v_ref[...] = v
    gate_ref[...] = _dot3f(jax.nn.sigmoid(_dot3f(xg, g1_ref[...])), g2_ref[...])
    kk = k * kk_ref_w[...]
    nrm = jnp.sqrt(_head_sum(kk * kk, ind_ref, indT_ref))
    kk = kk / jnp.maximum(nrm, 1e-12)
    kko_ref[...] = kk
    for d in range(2):
        z = dw0_ref[d:d + 1, :] + _dot3f(jnp.tanh(_dot3f(xw, dw1_ref[d])), dw2_ref[d])
        lw_ref[d] = -DECAY_SCALE * jax.nn.sigmoid(z)
        a = jax.nn.sigmoid(a0_ref[d:d + 1, :] + _dot3f(_dot3f(xa, a1_ref[d]), a2_ref[d]))
        kd_ref[d] = k * (1.0 + (a - 1.0) * ka_ref[...])
        beta_ref[d] = kk * a


def _head_indicator():
    ind = (np.arange(D_MODEL)[:, None] // RWKV_HEAD == np.arange(RWKV_HEADS)[None, :]).astype(np.float32)
    pad = np.zeros((D_MODEL, 128), np.float32)
    pad[:, :RWKV_HEADS] = ind
    return jnp.asarray(pad, BF16), jnp.asarray(pad.T, BF16)


def _rwkv_proj(h, flags, norm_g, mu, w_rkv, dec_w0, dec_w1, dec_w2, icl_a0, icl_a1, icl_a2,
               gate_g1, gate_g2, k_k, k_a):
    n_tok = h.shape[0]
    tile = PROJ_TILE
    wh, wl = _split(w_rkv)
    ind, indT = _head_indicator()
    consts = [norm_g.reshape(1, -1), mu, wh, wl, dec_w0, dec_w1, dec_w2, icl_a0, icl_a1, icl_a2,
              gate_g1, gate_g2, k_k.reshape(1, -1), k_a.reshape(1, -1), ind, indT]

    tok = pl.BlockSpec((tile, D_MODEL), lambda i, f: (i, 0))
    tok2 = pl.BlockSpec((2, tile, D_MODEL), lambda i, f: (0, i, 0))
    one = jax.ShapeDtypeStruct((n_tok, D_MODEL), F32)
    two = jax.ShapeDtypeStruct((2, n_tok, D_MODEL), F32)
    grid_spec = pltpu.PrefetchScalarGridSpec(
        num_scalar_prefetch=1,
        grid=(n_tok // tile,),
        in_specs=_halo_specs(tile, n_tok) + [_const_spec(a.shape) for a in consts],
        out_specs=[tok] * 5 + [tok2] * 3,
        scratch_shapes=[pltpu.VMEM((tile + 2 * HALO, D_MODEL), F32)],
    )
    return pl.pallas_call(
        _proj_kernel,
        grid_spec=grid_spec,
        out_shape=[one] * 5 + [two] * 3,
        compiler_params=_params(("parallel",)),
        name="rwkv_proj",
    )(flags, h, h, h, *consts)


def _scan_prep(rev, r_ref, lw_ref, kd_ref, kk_ref, beta_ref, incl):
    L = r_ref.shape[0]
    lw = lw_ref[...]
    lw_hi = lw.astype(BF16)
    lw_r = lw - lw_hi.astype(F32)
    lw_mid = lw_r.astype(BF16)
    lw_lo = (lw_r - lw_mid.astype(F32)).astype(BF16)
    inc_bf = incl.astype(BF16)
    cs = _dot(inc_bf, lw_hi) + (_dot(inc_bf, lw_mid) + _dot(inc_bf, lw_lo))
    end = 0 if rev else L - 1
    cs_end = cs[end:end + 1, :]
    g_inv = jnp.exp(-cs)
    g_rest = jnp.exp(cs_end - cs)
    beta = beta_ref[...]
    kd = kd_ref[...]
    return dict(
        alpha_b=-kk_ref[...] * jnp.exp(cs - lw), r_b=r_ref[...] * jnp.exp(cs),
        beta_t=beta * g_inv, k_t=kd * g_inv, beta_h=beta * g_rest, k_h=kd * g_rest,
        g_end=jnp.exp(cs_end))


def _scan_kernel(flags_ref, rf, lwf, kdf, vf, kkf, bf, rb, lwb, kdb, vb, kkb, bb, yf_ref, yb_ref, stf_ref, stb_ref):
    c = pl.program_id(0)
    n = pl.num_programs(0)
    L = rf.shape[0]
    N = RWKV_HEAD
    row = lax.broadcasted_iota(jnp.int32, (L, L), 0)
    col = lax.broadcasted_iota(jnp.int32, (L, L), 1)
    eye = (row == col).astype(F32)
    incl = (row >= col, row <= col)
    strict = (row > col, row < col)
    prep = (_scan_prep(False, rf, lwf, kdf, kkf, bf, incl[0]), _scan_prep(True, rb, lwb, kdb, kkb, bb, incl[1]))
    v_all = (vf[...], vb[...])
    reset = (flags_ref[0, c], flags_ref[1, n - 1 - c])
    st_refs = (stf_ref, stb_ref)
    y_refs = (yf_ref, yb_ref)
    chains = [(d, h) for h in range(RWKV_HEADS) for d in range(2)]

    def cols(d, h, name):
        return prep[d][name][:, h * N:(h + 1) * N]

    t0 = [jnp.where(reset[d] != 0, 0.0, st_refs[d][h]) for d, h in chains]
    x2 = [jnp.concatenate([cols(d, h, "alpha_b"), cols(d, h, "r_b")], axis=0) for d, h in chains]
    y2 = [jnp.concatenate([cols(d, h, "beta_t"), cols(d, h, "k_t")], axis=0) for d, h in chains]
    aa = [_dot3f(a, b, NT) for a, b in zip(x2, y2)]
    xt = [_dot1f(a, b) for a, b in zip(x2, t0)]
    a_ab = [jnp.where(strict[d], m[0:L, 0:L], 0.0) for (d, h), m in zip(chains, aa)]
    a_ak = [jnp.where(strict[d], m[0:L, L:2 * L], 0.0) for (d, h), m in zip(chains, aa)]
    a_r = [jnp.concatenate([jnp.where(incl[d], m[L:2 * L, 0:L], 0.0), jnp.where(incl[d], m[L:2 * L, L:2 * L], 0.0)],
                           axis=1) for (d, h), m in zip(chains, aa)]
    vh = [v_all[d][:, h * N:(h + 1) * N] for d, h in chains]
    rhs = [x[0:L] + _dot1f(a, b) for x, a, b in zip(xt, a_ak, vh)]
    minv = [eye + a for a in a_ab]
    ap = a_ab
    for _ in range(int(math.log2(L)) - 1):
        ap = [_dot1f(a, a) for a in ap]
        minv = [m + _dot1f(m, a) for m, a in zip(minv, ap)]
    u = [_dot1f(m, b) for m, b in zip(minv, rhs)]
    uv = [jnp.concatenate([a, b], axis=0) for a, b in zip(u, vh)]
    y = [x[L:2 * L] + _dot3f(a, b) for x, a, b in zip(xt, a_r, uv)]
    bk = [jnp.concatenate([cols(d, h, "beta_h"), cols(d, h, "k_h")], axis=0) for d, h in chains]
    t1 = [_dot1f(a, b, TN) for a, b in zip(bk, uv)]
    for i, (d, h) in enumerate(chains):
        y_refs[d][:, h * N:(h + 1) * N] = y[i]
        st_refs[d][h] = t0[i] * prep[d]["g_end"][:, h * N:(h + 1) * N].T + t1[i]


def _rwkv_scan(flags, r, lw, kd, v, kk, beta):
    n_tok = r.shape[0]
    L = SCAN_CHUNK
    n = n_tok // L
    fwd = pl.BlockSpec((L, D_MODEL), lambda c, f: (c, 0))
    bwd = pl.BlockSpec((L, D_MODEL), lambda c, f: (n - 1 - c, 0))

    def dspec(d, rev):
        return pl.BlockSpec((None, L, D_MODEL), (lambda c, f: (d, n - 1 - c, 0)) if rev else (lambda c, f: (d, c, 0)))

    grid_spec = pltpu.PrefetchScalarGridSpec(
        num_scalar_prefetch=1,
        grid=(n,),
        in_specs=[fwd, dspec(0, False), dspec(0, False), fwd, fwd, dspec(0, False),
                  bwd, dspec(1, True), dspec(1, True), bwd, bwd, dspec(1, True)],
        out_specs=[fwd, bwd],
        scratch_shapes=[pltpu.VMEM((RWKV_HEADS, RWKV_HEAD, RWKV_HEAD), F32)] * 2,
    )
    one = jax.ShapeDtypeStruct((n_tok, D_MODEL), F32)
    return pl.pallas_call(
        _scan_kernel,
        grid_spec=grid_spec,
        out_shape=[one, one],
        compiler_params=_params(("arbitrary",)),
        name="rwkv_scan",
    )(flags, r, lw, kd, v, kk, beta, r, lw, kd, v, kk, beta)


def _post_kernel(h_ref, yf_ref, yb_ref, r_ref, k_ref, v_ref, g_ref, rk_ref, lg_ref, lb_ref,
                 woh_ref, wol_ref, ind_ref, indT_ref, o_ref):
    y = yf_ref[...] + yb_ref[...]
    inv_n = 1.0 / RWKV_HEAD
    mean = _head_sum(y, ind_ref, indT_ref) * inv_n
    yc = y - mean
    var = _head_sum(yc * yc, ind_ref, indT_ref) * inv_n
    yn = yc * lax.rsqrt(var + LNX_EPS) * lg_ref[...] + lb_ref[...]
    bonus = _head_sum(r_ref[...] * k_ref[...] * rk_ref[...], ind_ref, indT_ref) * v_ref[...]
    out = _dot3((yn + bonus) * g_ref[...], woh_ref[...], wol_ref[...])
    o_ref[...] = h_ref[...] + out


def _rwkv_post(h, yf, yb, r, k, v, g, r_k, lnx_g, lnx_b, w_o):
    n_tok = h.shape[0]
    tile = POST_TILE
    woh, wol = _split(w_o)
    ind, indT = _head_indicator()
    tok = pl.BlockSpec((tile, D_MODEL), lambda i: (i, 0))
    consts = [r_k.reshape(1, -1), lnx_g.reshape(1, -1), lnx_b.reshape(1, -1), woh, wol, ind, indT]
    return pl.pallas_call(
        _post_kernel,
        grid=(n_tok // tile,),
        in_specs=[tok] * 7 + [_const_spec(a.shape) for a in consts],
        out_specs=tok,
        out_shape=jax.ShapeDtypeStruct((n_tok, D_MODEL), F32),
        compiler_params=_params(("parallel",)),
        name="rwkv_post",
    )(h, yf, yb, r, k, v, g, *consts)


def _rwkv_layer(h, seq_lens, norm_g, mu, w_rkv, w_o, dec_w0, dec_w1, dec_w2, icl_a0, icl_a1, icl_a2,
                gate_g1, gate_g2, k_k, k_a, r_k, lnx_g, lnx_b):
    r, k, v, g, kk, lw, kd, beta = _rwkv_proj(
        h, _tile_flags(seq_lens, PROJ_TILE), norm_g, mu, w_rkv, dec_w0, dec_w1, dec_w2,
        icl_a0, icl_a1, icl_a2, gate_g1, gate_g2, k_k, k_a)
    yf, yb = _rwkv_scan(_tile_flags(seq_lens, SCAN_CHUNK), r, lw, kd, v, kk, beta)
    return _rwkv_post(h, yf, yb, r, k, v, g, r_k, lnx_g, lnx_b, w_o)


def _trunk(h, seq_lens, norm_mix, norm_ffn, norm_final, pool_w, pool_scale,
           rwkv_mu, rwkv_w_rkv, rwkv_w_o, rwkv_dec_w0, rwkv_dec_w1, rwkv_dec_w2,
           rwkv_icl_a0, rwkv_icl_a1, rwkv_icl_a2, rwkv_gate_g1, rwkv_gate_g2,
           rwkv_k_k, rwkv_k_a, rwkv_r_k, rwkv_lnx_g, rwkv_lnx_b,
           peer_q, peer_subkeys, peer_u, peer_v):
    depth = norm_mix.shape[0]
    for i in range(depth):
        j = i // 2
        if i % 2 == 0:
            h = _pool_layer(h, _tile_flags(seq_lens, POOL_TILE), norm_mix[i], pool_w[j], pool_scale[j])
        else:
            h = _rwkv_layer(h, seq_lens, norm_mix[i], rwkv_mu[j], rwkv_w_rkv[j], rwkv_w_o[j],
                            rwkv_dec_w0[j], rwkv_dec_w1[j], rwkv_dec_w2[j], rwkv_icl_a0[j],
                            rwkv_icl_a1[j], rwkv_icl_a2[j], rwkv_gate_g1[j], rwkv_gate_g2[j],
                            rwkv_k_k[j], rwkv_k_a[j], rwkv_r_k[j].reshape(-1), rwkv_lnx_g[j], rwkv_lnx_b[j])
        h = _peer_layer(h, norm_ffn[i], peer_q[i], peer_subkeys[i], peer_u[i], peer_v[i],
                        norm_final, final_norm=(i == depth - 1))
    return h


def kernel(x_prompt, x_sample, norm_mix, norm_ffn, norm_final, pool_w, pool_scale, rwkv_mu, rwkv_w_rkv, rwkv_w_o, rwkv_dec_w0, rwkv_dec_w1, rwkv_dec_w2, rwkv_icl_a0, rwkv_icl_a1, rwkv_icl_a2, rwkv_gate_g1, rwkv_gate_g2, rwkv_k_k, rwkv_k_a, rwkv_r_k, rwkv_lnx_g, rwkv_lnx_b, peer_q, peer_subkeys, peer_u, peer_v):
    bp, sp, d = x_prompt.shape
    bs, ss, _ = x_sample.shape
    seq_lens = (sp,) * bp + (ss,) * bs
    h = jnp.concatenate([x_prompt.reshape(-1, d), x_sample.reshape(-1, d)], axis=0)
    y = _trunk(h, seq_lens, norm_mix, norm_ffn, norm_final, pool_w, pool_scale,
               rwkv_mu, rwkv_w_rkv, rwkv_w_o, rwkv_dec_w0, rwkv_dec_w1, rwkv_dec_w2,
               rwkv_icl_a0, rwkv_icl_a1, rwkv_icl_a2, rwkv_gate_g1, rwkv_gate_g2,
               rwkv_k_k, rwkv_k_a, rwkv_r_k, rwkv_lnx_g, rwkv_lnx_b,
               peer_q, peer_subkeys, peer_u, peer_v)
    n_p = bp * sp
    return (y[:n_p].reshape(bp, sp, d), y[n_p:].reshape(bs, ss, d))
```

```python
import functools
import math

import numpy as np
import jax
import jax.numpy as jnp
from jax import lax
from jax.experimental import pallas as pl
from jax.experimental.pallas import tpu as pltpu

F32 = jnp.float32
BF16 = jnp.bfloat16

D_MODEL = 1024
NORM_EPS = 1e-6
POOL_WINDOWS = (2, 4, 8, 16)
POOL_GROUP = D_MODEL // len(POOL_WINDOWS)
RWKV_HEAD = 64
RWKV_HEADS = D_MODEL // RWKV_HEAD
DECAY_SCALE = math.exp(-0.5)
LNX_EPS = 64e-5
PEER_HEADS = 8
PEER_KEYS = 128
PEER_EXPERTS = PEER_KEYS * PEER_KEYS
PEER_TOPK = 16
INV_SQRT2 = 0.7071067811865476

HALO = 8
VMEM_LIMIT = 56 * 1024 * 1024

POOL_TILE = 512
PROJ_TILE = 256
POST_TILE = 256
ROUTER_TILE = 256
DENSE_TILE = 512
DENSE_EBLK = 1024
DENSE_SUB = 256
SCAN_CHUNK = 64
NTOP = PEER_TOPK + 1
CAND_COUNTS = tuple(min(NTOP, NTOP // (i + 1)) for i in range(NTOP))
CAND_ROWS = -(-sum(CAND_COUNTS) // 8) * 8


def _params(sem):
    return pltpu.CompilerParams(dimension_semantics=sem, vmem_limit_bytes=VMEM_LIMIT)


def _split(w):
    hi = w.astype(BF16)
    lo = (w - hi.astype(F32)).astype(BF16)
    return hi, lo


def _dot(a, b, dims=(((1,), (0,)), ((), ()))):
    return lax.dot_general(a, b, dims, preferred_element_type=F32)


def _dot3(a, b_hi, b_lo, dims=(((1,), (0,)), ((), ()))):
    a_hi, a_lo = _split(a)
    return _dot(a_hi, b_hi, dims) + (_dot(a_lo, b_hi, dims) + _dot(a_hi, b_lo, dims))


def _dot2(a, b_exact, dims=(((1,), (0,)), ((), ()))):
    a_hi, a_lo = _split(a)
    return _dot(a_hi, b_exact, dims) + _dot(a_lo, b_exact, dims)


def _dot3f(a, b, dims=(((1,), (0,)), ((), ()))):
    b_hi, b_lo = _split(b)
    return _dot3(a, b_hi, b_lo, dims)


def _dot1f(a, b, dims=(((1,), (0,)), ((), ()))):
    return _dot(a.astype(BF16), b.astype(BF16), dims)


NT = (((1,), (1,)), ((), ()))
TN = (((0,), (0,)), ((), ()))


def _rmsnorm(x, g):
    return x * lax.rsqrt(jnp.mean(x * x, axis=-1, keepdims=True) + NORM_EPS) * g


def _tile_flags(seq_lens, tile):
    first, last = [], []
    for s in seq_lens:
        n = s // tile
        assert n * tile == s
        first += [1] + [0] * (n - 1)
        last += [0] * (n - 1) + [1]
    return jnp.asarray(np.array([first, last], np.int32))


def _halo_specs(tile, n_tok):
    r = tile // HALO
    nb = n_tok // HALO
    return [
        pl.BlockSpec((tile, D_MODEL), lambda i, f: (i, 0)),
        pl.BlockSpec((HALO, D_MODEL), lambda i, f: (jnp.maximum(i * r - 1, 0), 0)),
        pl.BlockSpec((HALO, D_MODEL), lambda i, f: (jnp.minimum((i + 1) * r, nb - 1), 0)),
    ]


def _fill_extended(xe_ref, u, up, un, first, last, tile):
    xe_ref[0:HALO, :] = jnp.where(first != 0, 0.0, up)
    xe_ref[HALO:HALO + tile, :] = u
    xe_ref[HALO + tile:2 * HALO + tile, :] = jnp.where(last != 0, 0.0, un)


def _const_spec(shape):
    nd = len(shape)
    return pl.BlockSpec(shape, lambda *_: (0,) * nd, pipeline_mode=pl.Buffered(1))


def _pool_kernel(flags_ref, x_ref, xp_ref, xn_ref, g_ref, pwh_ref, pwl_ref, ps_ref, o_ref, xe_ref):
    i = pl.program_id(0)
    tile = x_ref.shape[0]
    first = flags_ref[0, i]
    last = flags_ref[1, i]
    g = g_ref[...]
    x = x_ref[...]
    u = _rmsnorm(x, g)
    _fill_extended(xe_ref, u, _rmsnorm(xp_ref[...], g), _rmsnorm(xn_ref[...], g), first, last, tile)
    row = lax.broadcasted_iota(jnp.int32, (tile, 1), 0)
    for gi, w in enumerate(POOL_WINDOWS):
        sl = slice(gi * POOL_GROUP, (gi + 1) * POOL_GROUP)
        acc = xe_ref[pl.ds(HALO - w // 2, tile), sl]
        for j in range(-w // 2 + 1, w // 2):
            acc = acc + xe_ref[pl.ds(HALO + j, tile), sl]
        lo_clip = jnp.where(first != 0, jnp.maximum(w // 2 - row, 0), 0)
        hi_clip = jnp.where(last != 0, jnp.maximum(row + w // 2 - tile, 0), 0)
        cnt = (w - lo_clip - hi_clip).astype(F32)
        diff = acc / cnt - u[:, sl]
        y = _dot3(diff, pwh_ref[gi], pwl_ref[gi])
        o_ref[:, sl] = x[:, sl] + y * ps_ref[:, sl]


def _pool_layer(h, flags, norm_g, pool_w, pool_scale):
    n_tok = h.shape[0]
    tile = POOL_TILE
    pwh, pwl = _split(pool_w)
    grid_spec = pltpu.PrefetchScalarGridSpec(
        num_scalar_prefetch=1,
        grid=(n_tok // tile,),
        in_specs=_halo_specs(tile, n_tok) + [
            _const_spec((1, D_MODEL)), _const_spec(pwh.shape), _const_spec(pwl.shape),
            _const_spec((1, D_MODEL)),
        ],
        out_specs=pl.BlockSpec((tile, D_MODEL), lambda i, f: (i, 0)),
        scratch_shapes=[pltpu.VMEM((tile + 2 * HALO, D_MODEL), F32)],
    )
    return pl.pallas_call(
        _pool_kernel,
        grid_spec=grid_spec,
        out_shape=jax.ShapeDtypeStruct((n_tok, D_MODEL), F32),
        compiler_params=_params(("parallel",)),
        name="pool",
    )(flags, h, h, h, norm_g.reshape(1, -1), pwh, pwl, pool_scale.reshape(1, -1))


def _top_values(cur, n, out_ref, slot):
    rank = jnp.full(cur.shape, float(n), F32)
    for r in range(n):
        mx = jnp.max(cur, axis=0, keepdims=True)
        out_ref[slot, r:r + 1, :] = mx
        hit = cur == mx
        rank = jnp.where(hit, float(r), rank)
        cur = jnp.where(hit, -jnp.inf, cur)
    return rank


def _bf16_pair_words(x):
    bits = pltpu.bitcast(x.astype(BF16).astype(F32), jnp.uint32)
    return bits | lax.shift_right_logical(bits, jnp.uint32(16))


def _router_kernel(h_ref, g_ref, qwh_ref, qwl_ref, skh_ref, skl_ref,
                   hnT_ref, cnt_ref, e1_ref, rank_ref, e2_ref, s_ref, tops_ref, cand_ref):
    hn = _rmsnorm(h_ref[...], g_ref[...])
    hnT_ref[...] = hn.T.astype(BF16)
    q = _dot3(hn, qwh_ref[...], qwl_ref[...])
    for hp in range(2 * PEER_HEADS):
        qs = q[:, hp * PEER_KEYS:(hp + 1) * PEER_KEYS]
        qh, ql = _split(qs)
        skh = skh_ref[hp]
        s_ref[hp] = _dot(skh, qh, NT) + (_dot(skl_ref[hp], qh, NT) + _dot(skh, ql, NT))
    for h in range(PEER_HEADS):
        s1 = s_ref[2 * h]
        s2 = s_ref[2 * h + 1]
        _top_values(s1, NTOP, tops_ref, 0)
        rank = _top_values(s2, NTOP, tops_ref, 1)
        cand_ref[...] = jnp.full(cand_ref.shape, -jnp.inf, F32)
        off = 0
        for i, n in enumerate(CAND_COUNTS):
            cand_ref[0, off:off + n, :] = tops_ref[0, i:i + 1, :] + tops_ref[1, 0:n, :]
            off += n
        m1 = tops_ref[0, 0:1, :]
        m2 = tops_ref[1, 0:1, :]
        _top_values(cand_ref[0], NTOP, tops_ref, 2)
        top = tops_ref[2, 0:PEER_TOPK, :]
        z = jnp.sum(jnp.exp(top - top[0:1, :]), axis=0, keepdims=True)
        tau = 0.5 * (tops_ref[2, PEER_TOPK - 1:PEER_TOPK, :] + tops_ref[2, PEER_TOPK:PEER_TOPK + 1, :])
        thr = tau - s1
        cnt = jnp.zeros_like(s1)
        for k in range(NTOP):
            cnt = jnp.where(tops_ref[1, k:k + 1, :] >= thr, float(k + 1), cnt)
        cnt_ref[h] = _bf16_pair_words(cnt)
        e1_ref[h] = _bf16_pair_words(jnp.exp(s1 - m1) * (1.0 / z))
        e2 = jnp.exp(s2 - m2)
        for tc in range(rank_ref.shape[1]):
            cols = slice(tc * 128, (tc + 1) * 128)
            rank_ref[h, tc] = pltpu.bitcast(rank[:, cols].astype(BF16), jnp.uint32)
            e2_ref[h, tc] = pltpu.bitcast(e2[:, cols].astype(BF16), jnp.uint32)


def _peer_router(h, norm_g, q_w, subkeys):
    n_tok = h.shape[0]
    tile = ROUTER_TILE
    qwh, qwl = _split(q_w)
    sk = subkeys.reshape(2 * PEER_HEADS, PEER_KEYS, PEER_KEYS)
    skh, skl = _split(sk)
    rows = pl.BlockSpec((PEER_HEADS, PEER_KEYS, tile), lambda i: (0, 0, i))
    rows_shape = jax.ShapeDtypeStruct((PEER_HEADS, PEER_KEYS, n_tok), jnp.uint32)
    tiles = pl.BlockSpec((PEER_HEADS, tile // 128, PEER_KEYS // 2, 128), lambda i: (0, i, 0, 0))
    tiles_shape = jax.ShapeDtypeStruct((PEER_HEADS, n_tok // 128, PEER_KEYS // 2, 128), jnp.uint32)
    return pl.pallas_call(
        _router_kernel,
        grid=(n_tok // tile,),
        in_specs=[
            pl.BlockSpec((tile, D_MODEL), lambda i: (i, 0)),
            _const_spec((1, D_MODEL)),
            _const_spec(qwh.shape), _const_spec(qwl.shape),
            _const_spec(skh.shape), _const_spec(skl.shape),
        ],
        out_specs=[pl.BlockSpec((D_MODEL, tile), lambda i: (0, i)), rows, rows, tiles, tiles],
        out_shape=[jax.ShapeDtypeStruct((D_MODEL, n_tok), BF16), rows_shape, rows_shape, tiles_shape, tiles_shape],
        scratch_shapes=[pltpu.VMEM((2 * PEER_HEADS, PEER_KEYS, tile), F32),
                        pltpu.VMEM((3, 24, tile), F32), pltpu.VMEM((1, CAND_ROWS, tile), F32)],
        compiler_params=_params(("parallel",)),
        name="peer_router",
    )(h, norm_g.reshape(1, -1), qwh, qwl, skh, skl)


def _rows_as_bf16(words):
    return pltpu.bitcast(jnp.broadcast_to(words, (PEER_KEYS // 2, words.shape[1])), BF16)


def _dense_kernel(h_ref, hnT_ref, cnt_ref, e1_ref, rank_ref, e2_ref, u_ref, u_next_ref, vT_ref, vT_prev_ref, gf_ref,
                  o_ref, acc_ref, act_ref, p_ref, *, final_norm):
    j = pl.program_id(1)
    eblk, tile = u_ref.shape[0], hnT_ref.shape[1]
    n_sub = eblk // DENSE_SUB
    hnT = hnT_ref[...]

    @pl.when(j == 0)
    def _():
        acc_ref[...] = jnp.zeros(acc_ref.shape, F32)
        p_ref[1] = jnp.zeros(p_ref.shape[1:], BF16)
        act_ref[0] = _dot(u_ref[0:DENSE_SUB, :], hnT)

    for sb in range(n_sub):
        r0 = sb * DENSE_SUB
        cur, nxt = sb % 2, (sb + 1) % 2
        u_rows = u_ref[r0 + DENSE_SUB:r0 + 2 * DENSE_SUB, :] if sb + 1 < n_sub else u_next_ref[...]
        act_ref[nxt] = _dot(u_rows, hnT)
        v_cols = vT_ref[:, r0 - DENSE_SUB:r0] if sb > 0 else vT_prev_ref[...]
        acc_ref[...] += _dot(v_cols, p_ref[nxt])
        als = range(r0 // PEER_KEYS, (r0 + DENSE_SUB) // PEER_KEYS)
        for tc in range(tile // 128):
            cols = slice(tc * 128, (tc + 1) * 128)
            gates = [jnp.zeros((PEER_KEYS, 128), BF16) for _ in als]
            for h in range(PEER_HEADS):
                rank = pltpu.bitcast(rank_ref[h, tc], BF16)
                e2 = pltpu.bitcast(e2_ref[h, tc], BF16)
                for i, al in enumerate(als):
                    cnt_row = _rows_as_bf16(cnt_ref[h, al:al + 1, cols])
                    e1_row = _rows_as_bf16(e1_ref[h, al:al + 1, cols])
                    gates[i] = gates[i] + jnp.where(rank < cnt_row, e2 * e1_row, jnp.zeros((), BF16))
            for i, al in enumerate(als):
                rows = slice(al * PEER_KEYS - r0, (al + 1) * PEER_KEYS - r0)
                x = act_ref[cur, rows, cols]
                gelu = 0.5 * x * (1.0 + lax.erf(x * INV_SQRT2))
                p_ref[cur, rows, cols] = gates[i] * gelu.astype(BF16)

    @pl.when(j == pl.num_programs(1) - 1)
    def _():
        acc = acc_ref[...] + _dot(vT_ref[:, eblk - DENSE_SUB:eblk], p_ref[(n_sub - 1) % 2])
        out = h_ref[...] + acc.T
        if final_norm:
            out = _rmsnorm(out, gf_ref[...])
        o_ref[...] = out


def _peer_dense(h, hnT, cnt, e1, rank, e2, u_bf, vT_bf, final_g, final_norm):
    n_tok = h.shape[0]
    tile, eblk = DENSE_TILE, DENSE_EBLK
    n_sub = eblk // DENSE_SUB
    last_sub = PEER_EXPERTS // DENSE_SUB - 1
    assert n_sub % 2 == 0
    rows = pl.BlockSpec((PEER_HEADS, eblk // PEER_KEYS, tile), lambda i, j: (0, j, i))
    full = pl.BlockSpec((PEER_HEADS, tile // 128, PEER_KEYS // 2, 128), lambda i, j: (0, i, 0, 0))
    return pl.pallas_call(
        functools.partial(_dense_kernel, final_norm=final_norm),
        grid=(n_tok // tile, PEER_EXPERTS // eblk),
        in_specs=[
            pl.BlockSpec((tile, D_MODEL), lambda i, j: (i, 0)),
            pl.BlockSpec((D_MODEL, tile), lambda i, j: (0, i)),
            rows, rows, full, full,
            pl.BlockSpec((eblk, D_MODEL), lambda i, j: (j, 0)),
            pl.BlockSpec((DENSE_SUB, D_MODEL), lambda i, j: (jnp.minimum((j + 1) * n_sub, last_sub), 0)),
            pl.BlockSpec((D_MODEL, eblk), lambda i, j: (0, j)),
            pl.BlockSpec((D_MODEL, DENSE_SUB), lambda i, j: (0, jnp.maximum(j * n_sub - 1, 0))),
            _const_spec((1, D_MODEL)),
        ],
        out_specs=pl.BlockSpec((tile, D_MODEL), lambda i, j: (i, 0)),
        out_shape=jax.ShapeDtypeStruct((n_tok, D_MODEL), F32),
        scratch_shapes=[
            pltpu.VMEM((D_MODEL, tile), F32),
            pltpu.VMEM((2, DENSE_SUB, tile), F32),
            pltpu.VMEM((2, DENSE_SUB, tile), BF16),
        ],
        compiler_params=_params(("parallel", "arbitrary")),
        name="peer_dense",
    )(h, hnT, cnt, e1, rank, e2, u_bf, u_bf, vT_bf, vT_bf, final_g.reshape(1, -1))


def _peer_layer(h, norm_g, q_w, subkeys, u_tab, v_tab, final_g, final_norm):
    hnT, cnt, e1, rank, e2 = _peer_router(h, norm_g, q_w, subkeys)
    return _peer_dense(h, hnT, cnt, e1, rank, e2, u_tab.astype(BF16), v_tab.T.astype(BF16), final_g, final_norm)


def _head_sum(x, ind_ref, indT_ref):
    return _dot2(_dot2(x, ind_ref[...]), indT_ref[...])


def _proj_kernel(flags_ref, x_ref, xp_ref, xn_ref, g_ref, mu_ref, w_ref,
                 dw0_ref, dw1_ref, dw2_ref, a0_ref, a1_ref, a2_ref, g1_ref, g2_ref,
                 kk_ref_w, ka_ref, ind_ref, indT_ref,
                 r_ref, k_ref, v_ref, gate_ref, kko_ref, lw_ref, kd_ref, beta_ref, xe_ref):
    i = pl.program_id(0)
    tile = x_ref.shape[0]
    first = flags_ref[0, i]
    last = flags_ref[1, i]
    g = g_ref[...]
    u = _rmsnorm(x_ref[...], g)
    _fill_extended(xe_ref, u, _rmsnorm(xp_ref[...], g), _rmsnorm(xn_ref[...], g), first, last, tile)
    xx = 0.5 * (xe_ref[pl.ds(HALO - 1, tile), :] + xe_ref[pl.ds(HALO + 1, tile), :]) - u
    xr, xw, xk, xv, xa, xg = (u + xx * mu_ref[m:m + 1, :] for m in range(6))
    r = _dot(xr.astype(BF16), w_ref[0])
    k = _dot(xk.astype(BF16), w_ref[1])
    v = _dot(xv.astype(BF16), w_ref[2])
    r_ref[...] = r
    k_ref[...] = k
    v_ref[...] = v
    gate_ref[...] = _dot3f(jax.nn.sigmoid(_dot3f(xg, g1_ref[...])), g2_ref[...])
    kk = k * kk_ref_w[...]
    nrm = jnp.sqrt(_head_sum(kk * kk, ind_ref, indT_ref))
    kk = kk / jnp.maximum(nrm, 1e-12)
    kko_ref[...] = kk
    for d in range(2):
        z = dw0_ref[d:d + 1, :] + _dot3f(jnp.tanh(_dot3f(xw, dw1_ref[d])), dw2_ref[d])
        lw_ref[d] = -DECAY_SCALE * jax.nn.sigmoid(z)
        a = jax.nn.sigmoid(a0_ref[d:d + 1, :] + _dot3f(_dot3f(xa, a1_ref[d]), a2_ref[d]))
        kd_ref[d] = k * (1.0 + (a - 1.0) * ka_ref[...])
        beta_ref[d] = kk * a


def _head_indicator():
    ind = (np.arange(D_MODEL)[:, None] // RWKV_HEAD == np.arange(RWKV_HEADS)[None, :]).astype(np.float32)
    pad = np.zeros((D_MODEL, 128), np.float32)
    pad[:, :RWKV_HEADS] = ind
    return jnp.asarray(pad, BF16), jnp.asarray(pad.T, BF16)


def _rwkv_proj(h, flags, norm_g, mu, w_rkv, dec_w0, dec_w1, dec_w2, icl_a0, icl_a1, icl_a2,
               gate_g1, gate_g2, k_k, k_a):
    n_tok = h.shape[0]
    tile = PROJ_TILE
    ind, indT = _head_indicator()
    consts = [norm_g.reshape(1, -1), mu, w_rkv.astype(BF16), dec_w0, dec_w1, dec_w2, icl_a0, icl_a1, icl_a2,
              gate_g1, gate_g2, k_k.reshape(1, -1), k_a.reshape(1, -1), ind, indT]

    tok = pl.BlockSpec((tile, D_MODEL), lambda i, f: (i, 0))
    tok2 = pl.BlockSpec((2, tile, D_MODEL), lambda i, f: (0, i, 0))
    one = jax.ShapeDtypeStruct((n_tok, D_MODEL), F32)
    two = jax.ShapeDtypeStruct((2, n_tok, D_MODEL), F32)
    grid_spec = pltpu.PrefetchScalarGridSpec(
        num_scalar_prefetch=1,
        grid=(n_tok // tile,),
        in_specs=_halo_specs(tile, n_tok) + [_const_spec(a.shape) for a in consts],
        out_specs=[tok] * 5 + [tok2] * 3,
        scratch_shapes=[pltpu.VMEM((tile + 2 * HALO, D_MODEL), F32)],
    )
    return pl.pallas_call(
        _proj_kernel,
        grid_spec=grid_spec,
        out_shape=[one] * 5 + [two] * 3,
        compiler_params=_params(("parallel",)),
        name="rwkv_proj",
    )(flags, h, h, h, *consts)


def _scan_prep(rev, r_ref, lw_ref, kd_ref, kk_ref, beta_ref, incl):
    L = r_ref.shape[0]
    lw = lw_ref[...]
    lw_hi = lw.astype(BF16)
    lw_r = lw - lw_hi.astype(F32)
    lw_mid = lw_r.astype(BF16)
    lw_lo = (lw_r - lw_mid.astype(F32)).astype(BF16)
    inc_bf = incl.astype(BF16)
    cs = _dot(inc_bf, lw_hi) + (_dot(inc_bf, lw_mid) + _dot(inc_bf, lw_lo))
    end = 0 if rev else L - 1
    cs_end = cs[end:end + 1, :]
    g_inv = jnp.exp(-cs)
    g_rest = jnp.exp(cs_end - cs)
    beta = beta_ref[...]
    kd = kd_ref[...]
    return dict(
        alpha_b=-kk_ref[...] * jnp.exp(cs - lw), r_b=r_ref[...] * jnp.exp(cs),
        beta_t=beta * g_inv, k_t=kd * g_inv, beta_h=beta * g_rest, k_h=kd * g_rest,
        g_end=jnp.exp(cs_end))


def _scan_kernel(flags_ref, rf, lwf, kdf, vf, kkf, bf, rb, lwb, kdb, vb, kkb, bb, yf_ref, yb_ref, stf_ref, stb_ref):
    c = pl.program_id(0)
    n = pl.num_programs(0)
    L = rf.shape[0]
    N = RWKV_HEAD
    row = lax.broadcasted_iota(jnp.int32, (L, L), 0)
    col = lax.broadcasted_iota(jnp.int32, (L, L), 1)
    eye = (row == col).astype(F32)
    incl = (row >= col, row <= col)
    strict = (row > col, row < col)
    prep = (_scan_prep(False, rf, lwf, kdf, kkf, bf, incl[0]), _scan_prep(True, rb, lwb, kdb, kkb, bb, incl[1]))
    v_all = (vf[...], vb[...])
    reset = (flags_ref[0, c], flags_ref[1, n - 1 - c])
    st_refs = (stf_ref, stb_ref)
    y_refs = (yf_ref, yb_ref)
    chains = [(d, h) for h in range(RWKV_HEADS) for d in range(2)]

    def cols(d, h, name):
        return prep[d][name][:, h * N:(h + 1) * N]

    t0 = [jnp.where(reset[d] != 0, 0.0, st_refs[d][h]) for d, h in chains]
    x2 = [jnp.concatenate([cols(d, h, "alpha_b"), cols(d, h, "r_b")], axis=0) for d, h in chains]
    y2 = [jnp.concatenate([cols(d, h, "beta_t"), cols(d, h, "k_t")], axis=0) for d, h in chains]
    aa = [_dot1f(a, b, NT) for a, b in zip(x2, y2)]
    xt = [_dot1f(a, b) for a, b in zip(x2, t0)]
    a_ab = [jnp.where(strict[d], m[0:L, 0:L], 0.0) for (d, h), m in zip(chains, aa)]
    a_ak = [jnp.where(strict[d], m[0:L, L:2 * L], 0.0) for (d, h), m in zip(chains, aa)]
    a_r = [jnp.concatenate([jnp.where(incl[d], m[L:2 * L, 0:L], 0.0), jnp.where(incl[d], m[L:2 * L, L:2 * L], 0.0)],
                           axis=1) for (d, h), m in zip(chains, aa)]
    vh = [v_all[d][:, h * N:(h + 1) * N] for d, h in chains]
    rhs = [x[0:L] + _dot1f(a, b) for x, a, b in zip(xt, a_ak, vh)]
    minv = [eye + a for a in a_ab]
    ap = a_ab
    for _ in range(int(math.log2(L)) - 1):
        ap = [_dot1f(a, a) for a in ap]
        minv = [m + _dot1f(m, a) for m, a in zip(minv, ap)]
    u = [_dot1f(m, b) for m, b in zip(minv, rhs)]
    uv = [jnp.concatenate([a, b], axis=0) for a, b in zip(u, vh)]
    y = [x[L:2 * L] + _dot1f(a, b) for x, a, b in zip(xt, a_r, uv)]
    bk = [jnp.concatenate([cols(d, h, "beta_h"), cols(d, h, "k_h")], axis=0) for d, h in chains]
    t1 = [_dot1f(a, b, TN) for a, b in zip(bk, uv)]
    for i, (d, h) in enumerate(chains):
        y_refs[d][:, h * N:(h + 1) * N] = y[i]
        st_refs[d][h] = t0[i] * prep[d]["g_end"][:, h * N:(h + 1) * N].T + t1[i]


def _rwkv_scan(flags, r, lw, kd, v, kk, beta):
    n_tok = r.shape[0]
    L = SCAN_CHUNK
    n = n_tok // L
    fwd = pl.BlockSpec((L, D_MODEL), lambda c, f: (c, 0))
    bwd = pl.BlockSpec((L, D_MODEL), lambda c, f: (n - 1 - c, 0))

    def dspec(d, rev):
        return pl.BlockSpec((None, L, D_MODEL), (lambda c, f: (d, n - 1 - c, 0)) if rev else (lambda c, f: (d, c, 0)))

    grid_spec = pltpu.PrefetchScalarGridSpec(
        num_scalar_prefetch=1,
        grid=(n,),
        in_specs=[fwd, dspec(0, False), dspec(0, False), fwd, fwd, dspec(0, False),
                  bwd, dspec(1, True), dspec(1, True), bwd, bwd, dspec(1, True)],
        out_specs=[fwd, bwd],
        scratch_shapes=[pltpu.VMEM((RWKV_HEADS, RWKV_HEAD, RWKV_HEAD), F32)] * 2,
    )
    one = jax.ShapeDtypeStruct((n_tok, D_MODEL), F32)
    return pl.pallas_call(
        _scan_kernel,
        grid_spec=grid_spec,
        out_shape=[one, one],
        compiler_params=_params(("arbitrary",)),
        name="rwkv_scan",
    )(flags, r, lw, kd, v, kk, beta, r, lw, kd, v, kk, beta)


def _post_kernel(h_ref, yf_ref, yb_ref, r_ref, k_ref, v_ref, g_ref, rk_ref, lg_ref, lb_ref,
                 woh_ref, wol_ref, ind_ref, indT_ref, o_ref):
    y = yf_ref[...] + yb_ref[...]
    inv_n = 1.0 / RWKV_HEAD
    mean = _head_sum(y, ind_ref, indT_ref) * inv_n
    yc = y - mean
    var = _head_sum(yc * yc, ind_ref, indT_ref) * inv_n
    yn = yc * lax.rsqrt(var + LNX_EPS) * lg_ref[...] + lb_ref[...]
    bonus = _head_sum(r_ref[...] * k_ref[...] * rk_ref[...], ind_ref, indT_ref) * v_ref[...]
    out = _dot3((yn + bonus) * g_ref[...], woh_ref[...], wol_ref[...])
    o_ref[...] = h_ref[...] + out


def _rwkv_post(h, yf, yb, r, k, v, g, r_k, lnx_g, lnx_b, w_o):
    n_tok = h.shape[0]
    tile = POST_TILE
    woh, wol = _split(w_o)
    ind, indT = _head_indicator()
    tok = pl.BlockSpec((tile, D_MODEL), lambda i: (i, 0))
    consts = [r_k.reshape(1, -1), lnx_g.reshape(1, -1), lnx_b.reshape(1, -1), woh, wol, ind, indT]
    return pl.pallas_call(
        _post_kernel,
        grid=(n_tok // tile,),
        in_specs=[tok] * 7 + [_const_spec(a.shape) for a in consts],
        out_specs=tok,
        out_shape=jax.ShapeDtypeStruct((n_tok, D_MODEL), F32),
        compiler_params=_params(("parallel",)),
        name="rwkv_post",
    )(h, yf, yb, r, k, v, g, *consts)


def _rwkv_layer(h, seq_lens, norm_g, mu, w_rkv, w_o, dec_w0, dec_w1, dec_w2, icl_a0, icl_a1, icl_a2,
                gate_g1, gate_g2, k_k, k_a, r_k, lnx_g, lnx_b):
    r, k, v, g, kk, lw, kd, beta = _rwkv_proj(
        h, _tile_flags(seq_lens, PROJ_TILE), norm_g, mu, w_rkv, dec_w0, dec_w1, dec_w2,
        icl_a0, icl_a1, icl_a2, gate_g1, gate_g2, k_k, k_a)
    yf, yb = _rwkv_scan(_tile_flags(seq_lens, SCAN_CHUNK), r, lw, kd, v, kk, beta)
    return _rwkv_post(h, yf, yb, r, k, v, g, r_k, lnx_g, lnx_b, w_o)


def _trunk(h, seq_lens, norm_mix, norm_ffn, norm_final, pool_w, pool_scale,
           rwkv_mu, rwkv_w_rkv, rwkv_w_o, rwkv_dec_w0, rwkv_dec_w1, rwkv_dec_w2,
           rwkv_icl_a0, rwkv_icl_a1, rwkv_icl_a2, rwkv_gate_g1, rwkv_gate_g2,
           rwkv_k_k, rwkv_k_a, rwkv_r_k, rwkv_lnx_g, rwkv_lnx_b,
           peer_q, peer_subkeys, peer_u, peer_v):
    depth = norm_mix.shape[0]
    for i in range(depth):
        j = i // 2
        if i % 2 == 0:
            h = _pool_layer(h, _tile_flags(seq_lens, POOL_TILE), norm_mix[i], pool_w[j], pool_scale[j])
        else:
            h = _rwkv_layer(h, seq_lens, norm_mix[i], rwkv_mu[j], rwkv_w_rkv[j], rwkv_w_o[j],
                            rwkv_dec_w0[j], rwkv_dec_w1[j], rwkv_dec_w2[j], rwkv_icl_a0[j],
                            rwkv_icl_a1[j], rwkv_icl_a2[j], rwkv_gate_g1[j], rwkv_gate_g2[j],
                            rwkv_k_k[j], rwkv_k_a[j], rwkv_r_k[j].reshape(-1), rwkv_lnx_g[j], rwkv_lnx_b[j])
        h = _peer_layer(h, norm_ffn[i], peer_q[i], peer_subkeys[i], peer_u[i], peer_v[i],
                        norm_final, final_norm=(i == depth - 1))
    return h


def kernel(x_prompt, x_sample, norm_mix, norm_ffn, norm_final, pool_w, pool_scale, rwkv_mu, rwkv_w_rkv, rwkv_w_o, rwkv_dec_w0, rwkv_dec_w1, rwkv_dec_w2, rwkv_icl_a0, rwkv_icl_a1, rwkv_icl_a2, rwkv_gate_g1, rwkv_gate_g2, rwkv_k_k, rwkv_k_a, rwkv_r_k, rwkv_lnx_g, rwkv_lnx_b, peer_q, peer_subkeys, peer_u, peer_v):
    bp, sp, d = x_prompt.shape
    bs, ss, _ = x_sample.shape
    seq_lens = (sp,) * bp + (ss,) * bs
    h = jnp.concatenate([x_prompt.reshape(-1, d), x_sample.reshape(-1, d)], axis=0)
    y = _trunk(h, seq_lens, norm_mix, norm_ffn, norm_final, pool_w, pool_scale,
               rwkv_mu, rwkv_w_rkv, rwkv_w_o, rwkv_dec_w0, rwkv_dec_w1, rwkv_dec_w2,
               rwkv_icl_a0, rwkv_icl_a1, rwkv_icl_a2, rwkv_gate_g1, rwkv_gate_g2,
               rwkv_k_k, rwkv_k_a, rwkv_r_k, rwkv_lnx_g, rwkv_lnx_b,
               peer_q, peer_subkeys, peer_u, peer_v)
    n_p = bp * sp
    return (y[:n_p].reshape(bp, sp, d), y[n_p:].reshape(bs, ss, d))
```

```python
import functools
import math

import numpy as np
import jax
import jax.numpy as jnp
from jax import lax
from jax.experimental import pallas as pl
from jax.experimental.pallas import tpu as pltpu

F32 = jnp.float32
BF16 = jnp.bfloat16

D_MODEL = 1024
NORM_EPS = 1e-6
POOL_WINDOWS = (2, 4, 8, 16)
POOL_GROUP = D_MODEL // len(POOL_WINDOWS)
RWKV_HEAD = 64
RWKV_HEADS = D_MODEL // RWKV_HEAD
DECAY_SCALE = math.exp(-0.5)
LNX_EPS = 64e-5
DECAY_LORA = 64
PEER_HEADS = 8
PEER_KEYS = 128
PEER_EXPERTS = PEER_KEYS * PEER_KEYS
PEER_TOPK = 16
INV_SQRT2 = 0.7071067811865476

HALO = 8
VMEM_LIMIT = 56 * 1024 * 1024

POOL_TILE = 512
PROJ_TILE = 256
POST_TILE = 256
ROUTER_TILE = 256
DENSE_TILE = 1024
DENSE_EBLK = 1024
DENSE_SUB = 256
SCAN_CHUNK = 64
NTOP = PEER_TOPK + 1
CAND_COUNTS = tuple(min(NTOP, NTOP // (i + 1)) for i in range(NTOP))
CAND_ROWS = -(-sum(CAND_COUNTS) // 8) * 8


def _params(sem):
    return pltpu.CompilerParams(dimension_semantics=sem, vmem_limit_bytes=VMEM_LIMIT)


def _split(w):
    hi = w.astype(BF16)
    lo = (w - hi.astype(F32)).astype(BF16)
    return hi, lo


def _dot(a, b, dims=(((1,), (0,)), ((), ()))):
    return lax.dot_general(a, b, dims, preferred_element_type=F32)


def _dot3(a, b_hi, b_lo, dims=(((1,), (0,)), ((), ()))):
    a_hi, a_lo = _split(a)
    return _dot(a_hi, b_hi, dims) + (_dot(a_lo, b_hi, dims) + _dot(a_hi, b_lo, dims))


def _dot2(a, b_exact, dims=(((1,), (0,)), ((), ()))):
    a_hi, a_lo = _split(a)
    return _dot(a_hi, b_exact, dims) + _dot(a_lo, b_exact, dims)


def _dot3f(a, b, dims=(((1,), (0,)), ((), ()))):
    b_hi, b_lo = _split(b)
    return _dot3(a, b_hi, b_lo, dims)


def _dot1f(a, b, dims=(((1,), (0,)), ((), ()))):
    return _dot(a.astype(BF16), b.astype(BF16), dims)


NT = (((1,), (1,)), ((), ()))
TN = (((0,), (0,)), ((), ()))


def _rmsnorm(x, g):
    return x * lax.rsqrt(jnp.mean(x * x, axis=-1, keepdims=True) + NORM_EPS) * g


def _tile_flags(seq_lens, tile):
    first, last = [], []
    for s in seq_lens:
        n = s // tile
        assert n * tile == s
        first += [1] + [0] * (n - 1)
        last += [0] * (n - 1) + [1]
    return jnp.asarray(np.array([first, last], np.int32))


def _halo_specs(tile, n_tok):
    r = tile // HALO
    nb = n_tok // HALO
    return [
        pl.BlockSpec((tile, D_MODEL), lambda i, f: (i, 0)),
        pl.BlockSpec((HALO, D_MODEL), lambda i, f: (jnp.maximum(i * r - 1, 0), 0)),
        pl.BlockSpec((HALO, D_MODEL), lambda i, f: (jnp.minimum((i + 1) * r, nb - 1), 0)),
    ]


def _fill_extended(xe_ref, u, up, un, first, last, tile):
    xe_ref[0:HALO, :] = jnp.where(first != 0, 0.0, up)
    xe_ref[HALO:HALO + tile, :] = u
    xe_ref[HALO + tile:2 * HALO + tile, :] = jnp.where(last != 0, 0.0, un)


def _const_spec(shape):
    nd = len(shape)
    return pl.BlockSpec(shape, lambda *_: (0,) * nd, pipeline_mode=pl.Buffered(1))


def _pool_kernel(flags_ref, x_ref, xp_ref, xn_ref, g_ref, pwh_ref, pwl_ref, ps_ref, o_ref, xe_ref):
    i = pl.program_id(0)
    tile = x_ref.shape[0]
    first = flags_ref[0, i]
    last = flags_ref[1, i]
    g = g_ref[...]
    x = x_ref[...]
    u = _rmsnorm(x, g)
    _fill_extended(xe_ref, u, _rmsnorm(xp_ref[...], g), _rmsnorm(xn_ref[...], g), first, last, tile)
    row = lax.broadcasted_iota(jnp.int32, (tile, 1), 0)
    for gi, w in enumerate(POOL_WINDOWS):
        sl = slice(gi * POOL_GROUP, (gi + 1) * POOL_GROUP)
        acc = xe_ref[pl.ds(HALO - w // 2, tile), sl]
        for j in range(-w // 2 + 1, w // 2):
            acc = acc + xe_ref[pl.ds(HALO + j, tile), sl]
        lo_clip = jnp.where(first != 0, jnp.maximum(w // 2 - row, 0), 0)
        hi_clip = jnp.where(last != 0, jnp.maximum(row + w // 2 - tile, 0), 0)
        cnt = (w - lo_clip - hi_clip).astype(F32)
        diff = acc / cnt - u[:, sl]
        y = _dot3(diff, pwh_ref[gi], pwl_ref[gi])
        o_ref[:, sl] = x[:, sl] + y * ps_ref[:, sl]


def _pool_layer(h, flags, norm_g, pool_w, pool_scale):
    n_tok = h.shape[0]
    tile = POOL_TILE
    pwh, pwl = _split(pool_w)
    grid_spec = pltpu.PrefetchScalarGridSpec(
        num_scalar_prefetch=1,
        grid=(n_tok // tile,),
        in_specs=_halo_specs(tile, n_tok) + [
            _const_spec((1, D_MODEL)), _const_spec(pwh.shape), _const_spec(pwl.shape),
            _const_spec((1, D_MODEL)),
        ],
        out_specs=pl.BlockSpec((tile, D_MODEL), lambda i, f: (i, 0)),
        scratch_shapes=[pltpu.VMEM((tile + 2 * HALO, D_MODEL), F32)],
    )
    return pl.pallas_call(
        _pool_kernel,
        grid_spec=grid_spec,
        out_shape=jax.ShapeDtypeStruct((n_tok, D_MODEL), F32),
        compiler_params=_params(("parallel",)),
        name="pool",
    )(flags, h, h, h, norm_g.reshape(1, -1), pwh, pwl, pool_scale.reshape(1, -1))


def _top_values(cur, n, out_ref, slot):
    rank = jnp.full(cur.shape, float(n), F32)
    for r in range(n):
        mx = jnp.max(cur, axis=0, keepdims=True)
        out_ref[slot, r:r + 1, :] = mx
        hit = cur == mx
        rank = jnp.where(hit, float(r), rank)
        cur = jnp.where(hit, -jnp.inf, cur)
    return rank


def _bf16_pair_words(x):
    bits = pltpu.bitcast(x.astype(BF16).astype(F32), jnp.uint32)
    return bits | lax.shift_right_logical(bits, jnp.uint32(16))


def _router_kernel(h_ref, g_ref, qwh_ref, qwl_ref, skh_ref, skl_ref,
                   hnT_ref, cnt_ref, e1_ref, rank_ref, e2_ref, s_ref, tops_ref, cand_ref):
    hn = _rmsnorm(h_ref[...], g_ref[...])
    hnT_ref[...] = hn.T.astype(BF16)
    q = _dot3(hn, qwh_ref[...], qwl_ref[...])
    for hp in range(2 * PEER_HEADS):
        qs = q[:, hp * PEER_KEYS:(hp + 1) * PEER_KEYS]
        qh, ql = _split(qs)
        skh = skh_ref[hp]
        s_ref[hp] = _dot(skh, qh, NT) + (_dot(skl_ref[hp], qh, NT) + _dot(skh, ql, NT))
    for h in range(PEER_HEADS):
        s1 = s_ref[2 * h]
        s2 = s_ref[2 * h + 1]
        _top_values(s1, NTOP, tops_ref, 0)
        rank = _top_values(s2, NTOP, tops_ref, 1)
        cand_ref[...] = jnp.full(cand_ref.shape, -jnp.inf, F32)
        off = 0
        for i, n in enumerate(CAND_COUNTS):
            cand_ref[0, off:off + n, :] = tops_ref[0, i:i + 1, :] + tops_ref[1, 0:n, :]
            off += n
        m1 = tops_ref[0, 0:1, :]
        m2 = tops_ref[1, 0:1, :]
        _top_values(cand_ref[0], NTOP, tops_ref, 2)
        top = tops_ref[2, 0:PEER_TOPK, :]
        z = jnp.sum(jnp.exp(top - top[0:1, :]), axis=0, keepdims=True)
        tau = 0.5 * (tops_ref[2, PEER_TOPK - 1:PEER_TOPK, :] + tops_ref[2, PEER_TOPK:PEER_TOPK + 1, :])
        thr = tau - s1
        cnt = jnp.zeros_like(s1)
        for k in range(NTOP):
            cnt = jnp.where(tops_ref[1, k:k + 1, :] >= thr, float(k + 1), cnt)
        cnt_ref[h] = _bf16_pair_words(cnt)
        e1_ref[h] = _bf16_pair_words(jnp.exp(s1 - m1) * (1.0 / z))
        e2 = jnp.exp(s2 - m2)
        for tc in range(rank_ref.shape[1]):
            cols = slice(tc * 128, (tc + 1) * 128)
            rank_ref[h, tc] = pltpu.bitcast(rank[:, cols].astype(BF16), jnp.uint32)
            e2_ref[h, tc] = pltpu.bitcast(e2[:, cols].astype(BF16), jnp.uint32)


def _peer_router(h, norm_g, q_w, subkeys):
    n_tok = h.shape[0]
    tile = ROUTER_TILE
    qwh, qwl = _split(q_w)
    sk = subkeys.reshape(2 * PEER_HEADS, PEER_KEYS, PEER_KEYS)
    skh, skl = _split(sk)
    rows = pl.BlockSpec((PEER_HEADS, PEER_KEYS, tile), lambda i: (0, 0, i))
    rows_shape = jax.ShapeDtypeStruct((PEER_HEADS, PEER_KEYS, n_tok), jnp.uint32)
    tiles = pl.BlockSpec((PEER_HEADS, tile // 128, PEER_KEYS // 2, 128), lambda i: (0, i, 0, 0))
    tiles_shape = jax.ShapeDtypeStruct((PEER_HEADS, n_tok // 128, PEER_KEYS // 2, 128), jnp.uint32)
    return pl.pallas_call(
        _router_kernel,
        grid=(n_tok // tile,),
        in_specs=[
            pl.BlockSpec((tile, D_MODEL), lambda i: (i, 0)),
            _const_spec((1, D_MODEL)),
            _const_spec(qwh.shape), _const_spec(qwl.shape),
            _const_spec(skh.shape), _const_spec(skl.shape),
        ],
        out_specs=[pl.BlockSpec((D_MODEL, tile), lambda i: (0, i)), rows, rows, tiles, tiles],
        out_shape=[jax.ShapeDtypeStruct((D_MODEL, n_tok), BF16), rows_shape, rows_shape, tiles_shape, tiles_shape],
        scratch_shapes=[pltpu.VMEM((2 * PEER_HEADS, PEER_KEYS, tile), F32),
                        pltpu.VMEM((3, 24, tile), F32), pltpu.VMEM((1, CAND_ROWS, tile), F32)],
        compiler_params=_params(("parallel",)),
        name="peer_router",
    )(h, norm_g.reshape(1, -1), qwh, qwl, skh, skl)


def _rows_as_bf16(words):
    return pltpu.bitcast(jnp.broadcast_to(words, (PEER_KEYS // 2, words.shape[1])), BF16)


def _dense_kernel(h_ref, hnT_ref, cnt_ref, e1_ref, rank_ref, e2_ref, u_ref, u_next_ref, vT_ref, vT_prev_ref, gf_ref,
                  o_ref, acc_ref, act_ref, p_ref, *, final_norm):
    j = pl.program_id(1)
    eblk, tile = u_ref.shape[0], hnT_ref.shape[1]
    n_sub = eblk // DENSE_SUB
    hnT = hnT_ref[...]

    @pl.when(j == 0)
    def _():
        acc_ref[...] = jnp.zeros(acc_ref.shape, F32)
        p_ref[1] = jnp.zeros(p_ref.shape[1:], BF16)
        act_ref[0] = _dot(u_ref[0:DENSE_SUB, :], hnT)

    for sb in range(n_sub):
        r0 = sb * DENSE_SUB
        cur, nxt = sb % 2, (sb + 1) % 2
        u_rows = u_ref[r0 + DENSE_SUB:r0 + 2 * DENSE_SUB, :] if sb + 1 < n_sub else u_next_ref[...]
        v_cols = vT_ref[:, r0 - DENSE_SUB:r0] if sb > 0 else vT_prev_ref[...]
        for half in range(tile // 256):
            hc = slice(half * 256, (half + 1) * 256)
            act_ref[nxt, :, hc] = _dot(u_rows, hnT_ref[:, hc])
            acc_ref[:, hc] += _dot(v_cols, p_ref[nxt, :, hc])
        als = range(r0 // PEER_KEYS, (r0 + DENSE_SUB) // PEER_KEYS)
        for tc in range(tile // 128):
            cols = slice(tc * 128, (tc + 1) * 128)
            gates = [jnp.zeros((PEER_KEYS, 128), BF16) for _ in als]
            for h in range(PEER_HEADS):
                rank = pltpu.bitcast(rank_ref[h, tc], BF16)
                e2 = pltpu.bitcast(e2_ref[h, tc], BF16)
                for i, al in enumerate(als):
                    cnt_row = _rows_as_bf16(cnt_ref[h, al:al + 1, cols])
                    e1_row = _rows_as_bf16(e1_ref[h, al:al + 1, cols])
                    gates[i] = gates[i] + jnp.where(rank < cnt_row, e2 * e1_row, jnp.zeros((), BF16))
            for i, al in enumerate(als):
                rows = slice(al * PEER_KEYS - r0, (al + 1) * PEER_KEYS - r0)
                x = act_ref[cur, rows, cols]
                gelu = 0.5 * x * (1.0 + lax.erf(x * INV_SQRT2))
                p_ref[cur, rows, cols] = gates[i] * gelu.astype(BF16)

    @pl.when(j == pl.num_programs(1) - 1)
    def _():
        acc = acc_ref[...] + _dot(vT_ref[:, eblk - DENSE_SUB:eblk], p_ref[(n_sub - 1) % 2])
        out = h_ref[...] + acc.T
        if final_norm:
            out = _rmsnorm(out, gf_ref[...])
        o_ref[...] = out


def _peer_dense(h, hnT, cnt, e1, rank, e2, u_bf, vT_bf, final_g, final_norm):
    n_tok = h.shape[0]
    tile, eblk = DENSE_TILE, DENSE_EBLK
    n_sub = eblk // DENSE_SUB
    last_sub = PEER_EXPERTS // DENSE_SUB - 1
    assert n_sub % 2 == 0
    rows = pl.BlockSpec((PEER_HEADS, eblk // PEER_KEYS, tile), lambda i, j: (0, j, i))
    full = pl.BlockSpec((PEER_HEADS, tile // 128, PEER_KEYS // 2, 128), lambda i, j: (0, i, 0, 0))
    return pl.pallas_call(
        functools.partial(_dense_kernel, final_norm=final_norm),
        grid=(n_tok // tile, PEER_EXPERTS // eblk),
        in_specs=[
            pl.BlockSpec((tile, D_MODEL), lambda i, j: (i, 0)),
            pl.BlockSpec((D_MODEL, tile), lambda i, j: (0, i)),
            rows, rows, full, full,
            pl.BlockSpec((eblk, D_MODEL), lambda i, j: (j, 0)),
            pl.BlockSpec((DENSE_SUB, D_MODEL), lambda i, j: (jnp.minimum((j + 1) * n_sub, last_sub), 0)),
            pl.BlockSpec((D_MODEL, eblk), lambda i, j: (0, j)),
            pl.BlockSpec((D_MODEL, DENSE_SUB), lambda i, j: (0, jnp.maximum(j * n_sub - 1, 0))),
            _const_spec((1, D_MODEL)),
        ],
        out_specs=pl.BlockSpec((tile, D_MODEL), lambda i, j: (i, 0)),
        out_shape=jax.ShapeDtypeStruct((n_tok, D_MODEL), F32),
        scratch_shapes=[
            pltpu.VMEM((D_MODEL, tile), F32),
            pltpu.VMEM((2, DENSE_SUB, tile), F32),
            pltpu.VMEM((2, DENSE_SUB, tile), BF16),
        ],
        compiler_params=_params(("parallel", "arbitrary")),
        name="peer_dense",
    )(h, hnT, cnt, e1, rank, e2, u_bf, u_bf, vT_bf, vT_bf, final_g.reshape(1, -1))


def _peer_layer(h, norm_g, q_w, subkeys, u_tab, v_tab, final_g, final_norm):
    hnT, cnt, e1, rank, e2 = _peer_router(h, norm_g, q_w, subkeys)
    return _peer_dense(h, hnT, cnt, e1, rank, e2, u_tab.astype(BF16), v_tab.T.astype(BF16), final_g, final_norm)


def _head_sum(x, ind_ref, indT_ref):
    return _dot2(_dot2(x, ind_ref[...]), indT_ref[...])


def _proj_kernel(flags_ref, x_ref, xp_ref, xn_ref, g_ref, mu_ref, w_ref,
                 dw0_ref, dw1_ref, dw2_ref, a0_ref, a1_ref, a2_ref, g1_ref, g2_ref,
                 kk_ref_w, ka_ref, ind_ref, indT_ref,
                 r_ref, k_ref, v_ref, gate_ref, kko_ref, lw_ref, kd_ref, beta_ref, xe_ref):
    i = pl.program_id(0)
    tile = x_ref.shape[0]
    first = flags_ref[0, i]
    last = flags_ref[1, i]
    g = g_ref[...]
    u = _rmsnorm(x_ref[...], g)
    _fill_extended(xe_ref, u, _rmsnorm(xp_ref[...], g), _rmsnorm(xn_ref[...], g), first, last, tile)
    xx = 0.5 * (xe_ref[pl.ds(HALO - 1, tile), :] + xe_ref[pl.ds(HALO + 1, tile), :]) - u
    xr, xw, xk, xv, xa, xg = (u + xx * mu_ref[m:m + 1, :] for m in range(6))
    r = _dot(xr.astype(BF16), w_ref[0])
    k = _dot(xk.astype(BF16), w_ref[1])
    v = _dot(xv.astype(BF16), w_ref[2])
    r_ref[...] = r
    k_ref[...] = k
    v_ref[...] = v
    gate_ref[...] = _dot(jax.nn.sigmoid(_dot(xg.astype(BF16), g1_ref[...])).astype(BF16), g2_ref[...])
    kk = k * kk_ref_w[...]
    nrm = jnp.sqrt(_head_sum(kk * kk, ind_ref, indT_ref))
    kk = kk / jnp.maximum(nrm, 1e-12)
    kko_ref[...] = kk
    lora = DECAY_LORA
    tw = jnp.tanh(_dot(xw.astype(BF16), dw1_ref[...])).astype(BF16)
    aw = _dot(xa.astype(BF16), a1_ref[...]).astype(BF16)
    for d in range(2):
        z = dw0_ref[d:d + 1, :] + _dot(tw[:, d * lora:(d + 1) * lora], dw2_ref[d])
        lw_ref[d] = -DECAY_SCALE * jax.nn.sigmoid(z)
        a = jax.nn.sigmoid(a0_ref[d:d + 1, :] + _dot(aw[:, d * lora:(d + 1) * lora], a2_ref[d]))
        kd_ref[d] = k * (1.0 + (a - 1.0) * ka_ref[...])
        beta_ref[d] = kk * a


def _head_indicator():
    ind = (np.arange(D_MODEL)[:, None] // RWKV_HEAD == np.arange(RWKV_HEADS)[None, :]).astype(np.float32)
    pad = np.zeros((D_MODEL, 128), np.float32)
    pad[:, :RWKV_HEADS] = ind
    return jnp.asarray(pad, BF16), jnp.asarray(pad.T, BF16)


def _rwkv_proj(h, flags, norm_g, mu, w_rkv, dec_w0, dec_w1, dec_w2, icl_a0, icl_a1, icl_a2,
               gate_g1, gate_g2, k_k, k_a):
    n_tok = h.shape[0]
    tile = PROJ_TILE
    ind, indT = _head_indicator()
    assert dec_w1.shape[-1] == DECAY_LORA == icl_a1.shape[-1]
    side_by_side = lambda w: jnp.concatenate([w[0], w[1]], axis=1).astype(BF16)
    consts = [norm_g.reshape(1, -1), mu, w_rkv.astype(BF16), dec_w0, side_by_side(dec_w1), dec_w2.astype(BF16),
              icl_a0, side_by_side(icl_a1), icl_a2.astype(BF16), gate_g1.astype(BF16), gate_g2.astype(BF16),
              k_k.reshape(1, -1), k_a.reshape(1, -1), ind, indT]

    tok = pl.BlockSpec((tile, D_MODEL), lambda i, f: (i, 0))
    tok2 = pl.BlockSpec((2, tile, D_MODEL), lambda i, f: (0, i, 0))
    one = jax.ShapeDtypeStruct((n_tok, D_MODEL), F32)
    two = jax.ShapeDtypeStruct((2, n_tok, D_MODEL), F32)
    grid_spec = pltpu.PrefetchScalarGridSpec(
        num_scalar_prefetch=1,
        grid=(n_tok // tile,),
        in_specs=_halo_specs(tile, n_tok) + [_const_spec(a.shape) for a in consts],
        out_specs=[tok] * 5 + [tok2] * 3,
        scratch_shapes=[pltpu.VMEM((tile + 2 * HALO, D_MODEL), F32)],
    )
    return pl.pallas_call(
        _proj_kernel,
        grid_spec=grid_spec,
        out_shape=[one] * 5 + [two] * 3,
        compiler_params=_params(("parallel",)),
        name="rwkv_proj",
    )(flags, h, h, h, *consts)


def _scan_prep(rev, r_ref, lw_ref, kd_ref, kk_ref, beta_ref, incl):
    L = r_ref.shape[0]
    lw = lw_ref[...]
    lw_hi = lw.astype(BF16)
    lw_r = lw - lw_hi.astype(F32)
    lw_mid = lw_r.astype(BF16)
    lw_lo = (lw_r - lw_mid.astype(F32)).astype(BF16)
    inc_bf = incl.astype(BF16)
    cs = _dot(inc_bf, lw_hi) + (_dot(inc_bf, lw_mid) + _dot(inc_bf, lw_lo))
    end = 0 if rev else L - 1
    cs_end = cs[end:end + 1, :]
    g_inv = jnp.exp(-cs)
    g_rest = jnp.exp(cs_end - cs)
    beta = beta_ref[...]
    kd = kd_ref[...]
    return dict(
        alpha_b=-kk_ref[...] * jnp.exp(cs - lw), r_b=r_ref[...] * jnp.exp(cs),
        beta_t=beta * g_inv, k_t=kd * g_inv, beta_h=beta * g_rest, k_h=kd * g_rest,
        g_end=jnp.exp(cs_end))


def _scan_kernel(flags_ref, rf, lwf, kdf, vf, kkf, bf, rb, lwb, kdb, vb, kkb, bb, yf_ref, yb_ref, stf_ref, stb_ref):
    c = pl.program_id(0)
    n = pl.num_programs(0)
    L = rf.shape[0]
    N = RWKV_HEAD
    row = lax.broadcasted_iota(jnp.int32, (L, L), 0)
    col = lax.broadcasted_iota(jnp.int32, (L, L), 1)
    eye = (row == col).astype(F32)
    incl = (row >= col, row <= col)
    strict = (row > col, row < col)
    prep = (_scan_prep(False, rf, lwf, kdf, kkf, bf, incl[0]), _scan_prep(True, rb, lwb, kdb, kkb, bb, incl[1]))
    v_all = (vf[...], vb[...])
    reset = (flags_ref[0, c], flags_ref[1, n - 1 - c])
    st_refs = (stf_ref, stb_ref)
    y_refs = (yf_ref, yb_ref)
    chains = [(d, h) for h in range(RWKV_HEADS) for d in range(2)]

    def cols(d, h, name):
        return prep[d][name][:, h * N:(h + 1) * N]

    t0 = [jnp.where(reset[d] != 0, 0.0, st_refs[d][h]) for d, h in chains]
    x2 = [jnp.concatenate([cols(d, h, "alpha_b"), cols(d, h, "r_b")], axis=0) for d, h in chains]
    y2 = [jnp.concatenate([cols(d, h, "beta_t"), cols(d, h, "k_t")], axis=0) for d, h in chains]
    aa = [_dot1f(a, b, NT) for a, b in zip(x2, y2)]
    xt = [_dot1f(a, b) for a, b in zip(x2, t0)]
    a_ab = [jnp.where(strict[d], m[0:L, 0:L], 0.0) for (d, h), m in zip(chains, aa)]
    a_ak = [jnp.where(strict[d], m[0:L, L:2 * L], 0.0) for (d, h), m in zip(chains, aa)]
    a_r = [jnp.concatenate([jnp.where(incl[d], m[L:2 * L, 0:L], 0.0), jnp.where(incl[d], m[L:2 * L, L:2 * L], 0.0)],
                           axis=1) for (d, h), m in zip(chains, aa)]
    vh = [v_all[d][:, h * N:(h + 1) * N] for d, h in chains]
    rhs = [x[0:L] + _dot1f(a, b) for x, a, b in zip(xt, a_ak, vh)]
    minv = [eye + a for a in a_ab]
    ap = a_ab
    for _ in range(int(math.log2(L)) - 1):
        ap = [_dot1f(a, a) for a in ap]
        minv = [m + _dot1f(m, a) for m, a in zip(minv, ap)]
    u = [_dot1f(m, b) for m, b in zip(minv, rhs)]
    uv = [jnp.concatenate([a, b], axis=0) for a, b in zip(u, vh)]
    y = [x[L:2 * L] + _dot1f(a, b) for x, a, b in zip(xt, a_r, uv)]
    bk = [jnp.concatenate([cols(d, h, "beta_h"), cols(d, h, "k_h")], axis=0) for d, h in chains]
    t1 = [_dot1f(a, b, TN) for a, b in zip(bk, uv)]
    for i, (d, h) in enumerate(chains):
        y_refs[d][:, h * N:(h + 1) * N] = y[i]
        st_refs[d][h] = t0[i] * prep[d]["g_end"][:, h * N:(h + 1) * N].T + t1[i]


def _rwkv_scan(flags, r, lw, kd, v, kk, beta):
    n_tok = r.shape[0]
    L = SCAN_CHUNK
    n = n_tok // L
    fwd = pl.BlockSpec((L, D_MODEL), lambda c, f: (c, 0))
    bwd = pl.BlockSpec((L, D_MODEL), lambda c, f: (n - 1 - c, 0))

    def dspec(d, rev):
        return pl.BlockSpec((None, L, D_MODEL), (lambda c, f: (d, n - 1 - c, 0)) if rev else (lambda c, f: (d, c, 0)))

    grid_spec = pltpu.PrefetchScalarGridSpec(
        num_scalar_prefetch=1,
        grid=(n,),
        in_specs=[fwd, dspec(0, False), dspec(0, False), fwd, fwd, dspec(0, False),
                  bwd, dspec(1, True), dspec(1, True), bwd, bwd, dspec(1, True)],
        out_specs=[fwd, bwd],
        scratch_shapes=[pltpu.VMEM((RWKV_HEADS, RWKV_HEAD, RWKV_HEAD), F32)] * 2,
    )
    one = jax.ShapeDtypeStruct((n_tok, D_MODEL), F32)
    return pl.pallas_call(
        _scan_kernel,
        grid_spec=grid_spec,
        out_shape=[one, one],
        compiler_params=_params(("arbitrary",)),
        name="rwkv_scan",
    )(flags, r, lw, kd, v, kk, beta, r, lw, kd, v, kk, beta)


def _post_kernel(h_ref, yf_ref, yb_ref, r_ref, k_ref, v_ref, g_ref, rk_ref, lg_ref, lb_ref,
                 wo_ref, ind_ref, indT_ref, o_ref):
    y = yf_ref[...] + yb_ref[...]
    inv_n = 1.0 / RWKV_HEAD
    mean = _head_sum(y, ind_ref, indT_ref) * inv_n
    yc = y - mean
    var = _head_sum(yc * yc, ind_ref, indT_ref) * inv_n
    yn = yc * lax.rsqrt(var + LNX_EPS) * lg_ref[...] + lb_ref[...]
    bonus = _head_sum(r_ref[...] * k_ref[...] * rk_ref[...], ind_ref, indT_ref) * v_ref[...]
    out = _dot(((yn + bonus) * g_ref[...]).astype(BF16), wo_ref[...])
    o_ref[...] = h_ref[...] + out


def _rwkv_post(h, yf, yb, r, k, v, g, r_k, lnx_g, lnx_b, w_o):
    n_tok = h.shape[0]
    tile = POST_TILE
    ind, indT = _head_indicator()
    tok = pl.BlockSpec((tile, D_MODEL), lambda i: (i, 0))
    consts = [r_k.reshape(1, -1), lnx_g.reshape(1, -1), lnx_b.reshape(1, -1), w_o.astype(BF16), ind, indT]
    return pl.pallas_call(
        _post_kernel,
        grid=(n_tok // tile,),
        in_specs=[tok] * 7 + [_const_spec(a.shape) for a in consts],
        out_specs=tok,
        out_shape=jax.ShapeDtypeStruct((n_tok, D_MODEL), F32),
        compiler_params=_params(("parallel",)),
        name="rwkv_post",
    )(h, yf, yb, r, k, v, g, *consts)


def _rwkv_layer(h, seq_lens, norm_g, mu, w_rkv, w_o, dec_w0, dec_w1, dec_w2, icl_a0, icl_a1, icl_a2,
                gate_g1, gate_g2, k_k, k_a, r_k, lnx_g, lnx_b):
    r, k, v, g, kk, lw, kd, beta = _rwkv_proj(
        h, _tile_flags(seq_lens, PROJ_TILE), norm_g, mu, w_rkv, dec_w0, dec_w1, dec_w2,
        icl_a0, icl_a1, icl_a2, gate_g1, gate_g2, k_k, k_a)
    yf, yb = _rwkv_scan(_tile_flags(seq_lens, SCAN_CHUNK), r, lw, kd, v, kk, beta)
    return _rwkv_post(h, yf, yb, r, k, v, g, r_k, lnx_g, lnx_b, w_o)


def _trunk(h, seq_lens, norm_mix, norm_ffn, norm_final, pool_w, pool_scale,
           rwkv_mu, rwkv_w_rkv, rwkv_w_o, rwkv_dec_w0, rwkv_dec_w1, rwkv_dec_w2,
           rwkv_icl_a0, rwkv_icl_a1, rwkv_icl_a2, rwkv_gate_g1, rwkv_gate_g2,
           rwkv_k_k, rwkv_k_a, rwkv_r_k, rwkv_lnx_g, rwkv_lnx_b,
           peer_q, peer_subkeys, peer_u, peer_v):
    depth = norm_mix.shape[0]
    for i in range(depth):
        j = i // 2
        if i % 2 == 0:
            h = _pool_layer(h, _tile_flags(seq_lens, POOL_TILE), norm_mix[i], pool_w[j], pool_scale[j])
        else:
            h = _rwkv_layer(h, seq_lens, norm_mix[i], rwkv_mu[j], rwkv_w_rkv[j], rwkv_w_o[j],
                            rwkv_dec_w0[j], rwkv_dec_w1[j], rwkv_dec_w2[j], rwkv_icl_a0[j],
                            rwkv_icl_a1[j], rwkv_icl_a2[j], rwkv_gate_g1[j], rwkv_gate_g2[j],
                            rwkv_k_k[j], rwkv_k_a[j], rwkv_r_k[j].reshape(-1), rwkv_lnx_g[j], rwkv_lnx_b[j])
        h = _peer_layer(h, norm_ffn[i], peer_q[i], peer_subkeys[i], peer_u[i], peer_v[i],
                        norm_final, final_norm=(i == depth - 1))
    return h


def kernel(x_prompt, x_sample, norm_mix, norm_ffn, norm_final, pool_w, pool_scale, rwkv_mu, rwkv_w_rkv, rwkv_w_o, rwkv_dec_w0, rwkv_dec_w1, rwkv_dec_w2, rwkv_icl_a0, rwkv_icl_a1, rwkv_icl_a2, rwkv_gate_g1, rwkv_gate_g2, rwkv_k_k, rwkv_k_a, rwkv_r_k, rwkv_lnx_g, rwkv_lnx_b, peer_q, peer_subkeys, peer_u, peer_v):
    bp, sp, d = x_prompt.shape
    bs, ss, _ = x_sample.shape
    seq_lens = (sp,) * bp + (ss,) * bs
    h = jnp.concatenate([x_prompt.reshape(-1, d), x_sample.reshape(-1, d)], axis=0)
    y = _trunk(h, seq_lens, norm_mix, norm_ffn, norm_final, pool_w, pool_scale,
               rwkv_mu, rwkv_w_rkv, rwkv_w_o, rwkv_dec_w0, rwkv_dec_w1, rwkv_dec_w2,
               rwkv_icl_a0, rwkv_icl_a1, rwkv_icl_a2, rwkv_gate_g1, rwkv_gate_g2,
               rwkv_k_k, rwkv_k_a, rwkv_r_k, rwkv_lnx_g, rwkv_lnx_b,
               peer_q, peer_subkeys, peer_u, peer_v)
    n_p = bp * sp
    return (y[:n_p].reshape(bp, sp, d), y[n_p:].reshape(bs, ss, d))
```

```python
import functools
import math

import numpy as np
import jax
import jax.numpy as jnp
from jax import lax
from jax.experimental import pallas as pl
from jax.experimental.pallas import tpu as pltpu

F32 = jnp.float32
BF16 = jnp.bfloat16

D_MODEL = 1024
NORM_EPS = 1e-6
POOL_WINDOWS = (2, 4, 8, 16)
POOL_GROUP = D_MODEL // len(POOL_WINDOWS)
RWKV_HEAD = 64
RWKV_HEADS = D_MODEL // RWKV_HEAD
DECAY_SCALE = math.exp(-0.5)
LNX_EPS = 64e-5
DECAY_LORA = 64
PEER_HEADS = 8
PEER_KEYS = 128
PEER_EXPERTS = PEER_KEYS * PEER_KEYS
PEER_TOPK = 16
INV_SQRT2 = 0.7071067811865476

HALO = 8
VMEM_LIMIT = 56 * 1024 * 1024

POOL_TILE = 512
PROJ_TILE = 256
POST_TILE = 256
ROUTER_TILE = 256
DENSE_TILE = 1024
DENSE_EBLK = 1024
DENSE_SUB = 256
SCAN_CHUNK = 64
NTOP = PEER_TOPK + 1
CAND_COUNTS = tuple(min(NTOP, NTOP // (i + 1)) for i in range(NTOP))
CAND_ROWS = -(-sum(CAND_COUNTS) // 8) * 8


def _params(sem):
    return pltpu.CompilerParams(dimension_semantics=sem, vmem_limit_bytes=VMEM_LIMIT)


def _split(w):
    hi = w.astype(BF16)
    lo = (w - hi.astype(F32)).astype(BF16)
    return hi, lo


def _dot(a, b, dims=(((1,), (0,)), ((), ()))):
    return lax.dot_general(a, b, dims, preferred_element_type=F32)


def _dot3(a, b_hi, b_lo, dims=(((1,), (0,)), ((), ()))):
    a_hi, a_lo = _split(a)
    return _dot(a_hi, b_hi, dims) + (_dot(a_lo, b_hi, dims) + _dot(a_hi, b_lo, dims))


def _dot2(a, b_exact, dims=(((1,), (0,)), ((), ()))):
    a_hi, a_lo = _split(a)
    return _dot(a_hi, b_exact, dims) + _dot(a_lo, b_exact, dims)


def _dot3f(a, b, dims=(((1,), (0,)), ((), ()))):
    b_hi, b_lo = _split(b)
    return _dot3(a, b_hi, b_lo, dims)


def _dot1f(a, b, dims=(((1,), (0,)), ((), ()))):
    return _dot(a.astype(BF16), b.astype(BF16), dims)


NT = (((1,), (1,)), ((), ()))
TN = (((0,), (0,)), ((), ()))


def _rmsnorm(x, g):
    return x * lax.rsqrt(jnp.mean(x * x, axis=-1, keepdims=True) + NORM_EPS) * g


def _tile_flags(seq_lens, tile):
    first, last = [], []
    for s in seq_lens:
        n = s // tile
        assert n * tile == s
        first += [1] + [0] * (n - 1)
        last += [0] * (n - 1) + [1]
    return jnp.asarray(np.array([first, last], np.int32))


def _halo_specs(tile, n_tok):
    r = tile // HALO
    nb = n_tok // HALO
    return [
        pl.BlockSpec((tile, D_MODEL), lambda i, f: (i, 0)),
        pl.BlockSpec((HALO, D_MODEL), lambda i, f: (jnp.maximum(i * r - 1, 0), 0)),
        pl.BlockSpec((HALO, D_MODEL), lambda i, f: (jnp.minimum((i + 1) * r, nb - 1), 0)),
    ]


def _fill_extended(xe_ref, u, up, un, first, last, tile):
    xe_ref[0:HALO, :] = jnp.where(first != 0, 0.0, up)
    xe_ref[HALO:HALO + tile, :] = u
    xe_ref[HALO + tile:2 * HALO + tile, :] = jnp.where(last != 0, 0.0, un)


def _const_spec(shape):
    nd = len(shape)
    return pl.BlockSpec(shape, lambda *_: (0,) * nd, pipeline_mode=pl.Buffered(1))


def _pool_kernel(flags_ref, x_ref, xp_ref, xn_ref, g_ref, pwh_ref, pwl_ref, ps_ref, o_ref, xe_ref):
    i = pl.program_id(0)
    tile = x_ref.shape[0]
    first = flags_ref[0, i]
    last = flags_ref[1, i]
    g = g_ref[...]
    x = x_ref[...]
    u = _rmsnorm(x, g)
    _fill_extended(xe_ref, u, _rmsnorm(xp_ref[...], g), _rmsnorm(xn_ref[...], g), first, last, tile)
    row = lax.broadcasted_iota(jnp.int32, (tile, 1), 0)
    for gi, w in enumerate(POOL_WINDOWS):
        sl = slice(gi * POOL_GROUP, (gi + 1) * POOL_GROUP)
        acc = xe_ref[pl.ds(HALO - w // 2, tile), sl]
        for j in range(-w // 2 + 1, w // 2):
            acc = acc + xe_ref[pl.ds(HALO + j, tile), sl]
        lo_clip = jnp.where(first != 0, jnp.maximum(w // 2 - row, 0), 0)
        hi_clip = jnp.where(last != 0, jnp.maximum(row + w // 2 - tile, 0), 0)
        cnt = (w - lo_clip - hi_clip).astype(F32)
        diff = acc / cnt - u[:, sl]
        y = _dot3(diff, pwh_ref[gi], pwl_ref[gi])
        o_ref[:, sl] = x[:, sl] + y * ps_ref[:, sl]


def _pool_layer(h, flags, norm_g, pool_w, pool_scale):
    n_tok = h.shape[0]
    tile = POOL_TILE
    pwh, pwl = _split(pool_w)
    grid_spec = pltpu.PrefetchScalarGridSpec(
        num_scalar_prefetch=1,
        grid=(n_tok // tile,),
        in_specs=_halo_specs(tile, n_tok) + [
            _const_spec((1, D_MODEL)), _const_spec(pwh.shape), _const_spec(pwl.shape),
            _const_spec((1, D_MODEL)),
        ],
        out_specs=pl.BlockSpec((tile, D_MODEL), lambda i, f: (i, 0)),
        scratch_shapes=[pltpu.VMEM((tile + 2 * HALO, D_MODEL), F32)],
    )
    return pl.pallas_call(
        _pool_kernel,
        grid_spec=grid_spec,
        out_shape=jax.ShapeDtypeStruct((n_tok, D_MODEL), F32),
        compiler_params=_params(("parallel",)),
        name="pool",
    )(flags, h, h, h, norm_g.reshape(1, -1), pwh, pwl, pool_scale.reshape(1, -1))


def _top_values(cur, n, out_ref, slot):
    for r in range(n):
        mx = jnp.max(cur, axis=0, keepdims=True)
        out_ref[slot, r:r + 1, :] = mx
        cur = jnp.where(cur == mx, -jnp.inf, cur)


def _sort_network(n):
    pairs = []

    def merge(lo, hi, r):
        step = r * 2
        if step < hi - lo:
            merge(lo, hi, step)
            merge(lo + r, hi, step)
            pairs.extend((i, i + r) for i in range(lo + r, hi - r, step))
        else:
            pairs.append((lo, lo + r))

    def sort(lo, hi):
        if hi - lo >= 1:
            mid = lo + (hi - lo) // 2
            sort(lo, mid)
            sort(mid + 1, hi)
            merge(lo, hi, 1)

    sort(0, n - 1)
    return pairs


def _top_values_sorted(s, n, out_ref, slot):
    groups = PEER_KEYS // 8
    v = [s[8 * k:8 * (k + 1), :] for k in range(groups)]
    for i, j in _sort_network(groups):
        v[i], v[j] = jnp.maximum(v[i], v[j]), jnp.minimum(v[i], v[j])
    for r in range(n):
        mx = jnp.max(v[0], axis=0, keepdims=True)
        out_ref[slot, r:r + 1, :] = mx
        hit = v[0] == mx
        for k in range(min(groups, n - r - 1)):
            v[k] = jnp.where(hit, v[k + 1] if k + 1 < groups else -jnp.inf, v[k])


def _bf16_pair_words(x):
    bits = pltpu.bitcast(x.astype(BF16).astype(F32), jnp.uint32)
    return bits | lax.shift_right_logical(bits, jnp.uint32(16))


def _router_kernel(h_ref, g_ref, qwh_ref, qwl_ref, skh_ref, skl_ref,
                   hnT_ref, cnt_ref, e1_ref, rank_ref, e2_ref, s_ref, tops_ref, cand_ref):
    hn = _rmsnorm(h_ref[...], g_ref[...])
    hnT_ref[...] = hn.T.astype(BF16)
    q = _dot3(hn, qwh_ref[...], qwl_ref[...])
    for hp in range(2 * PEER_HEADS):
        qs = q[:, hp * PEER_KEYS:(hp + 1) * PEER_KEYS]
        qh, ql = _split(qs)
        skh = skh_ref[hp]
        s_ref[hp] = _dot(skh, qh, NT) + (_dot(skl_ref[hp], qh, NT) + _dot(skh, ql, NT))
    for h in range(PEER_HEADS):
        s1 = s_ref[2 * h]
        s2 = s_ref[2 * h + 1]
        _top_values_sorted(s1, NTOP, tops_ref, 0)
        _top_values_sorted(s2, NTOP, tops_ref, 1)
        cand_ref[...] = jnp.full(cand_ref.shape, -jnp.inf, F32)
        off = 0
        for i, n in enumerate(CAND_COUNTS):
            cand_ref[0, off:off + n, :] = tops_ref[0, i:i + 1, :] + tops_ref[1, 0:n, :]
            off += n
        m1 = tops_ref[0, 0:1, :]
        m2 = tops_ref[1, 0:1, :]
        _top_values(cand_ref[0], NTOP, tops_ref, 2)
        top = tops_ref[2, 0:PEER_TOPK, :]
        z = jnp.sum(jnp.exp(top - top[0:1, :]), axis=0, keepdims=True)
        tau = 0.5 * (tops_ref[2, PEER_TOPK - 1:PEER_TOPK, :] + tops_ref[2, PEER_TOPK:PEER_TOPK + 1, :])
        thr = tau - s1
        cnt = jnp.zeros_like(s1)
        rank = jnp.zeros_like(s2)
        for k in range(NTOP):
            tk = tops_ref[1, k:k + 1, :]
            cnt = jnp.where(tk >= thr, float(k + 1), cnt)
            rank = jnp.where(tk > s2, float(k + 1), rank)
        cnt_ref[h] = _bf16_pair_words(cnt)
        e1_ref[h] = _bf16_pair_words(jnp.exp(s1 - m1) * (1.0 / z))
        e2 = jnp.exp(s2 - m2)
        for tc in range(rank_ref.shape[1]):
            cols = slice(tc * 128, (tc + 1) * 128)
            rank_ref[h, tc] = pltpu.bitcast(rank[:, cols].astype(BF16), jnp.uint32)
            e2_ref[h, tc] = pltpu.bitcast(e2[:, cols].astype(BF16), jnp.uint32)


def _peer_router(h, norm_g, q_w, subkeys):
    n_tok = h.shape[0]
    tile = ROUTER_TILE
    qwh, qwl = _split(q_w)
    sk = subkeys.reshape(2 * PEER_HEADS, PEER_KEYS, PEER_KEYS)
    skh, skl = _split(sk)
    rows = pl.BlockSpec((PEER_HEADS, PEER_KEYS, tile), lambda i: (0, 0, i))
    rows_shape = jax.ShapeDtypeStruct((PEER_HEADS, PEER_KEYS, n_tok), jnp.uint32)
    tiles = pl.BlockSpec((PEER_HEADS, tile // 128, PEER_KEYS // 2, 128), lambda i: (0, i, 0, 0))
    tiles_shape = jax.ShapeDtypeStruct((PEER_HEADS, n_tok // 128, PEER_KEYS // 2, 128), jnp.uint32)
    return pl.pallas_call(
        _router_kernel,
        grid=(n_tok // tile,),
        in_specs=[
            pl.BlockSpec((tile, D_MODEL), lambda i: (i, 0)),
            _const_spec((1, D_MODEL)),
            _const_spec(qwh.shape), _const_spec(qwl.shape),
            _const_spec(skh.shape), _const_spec(skl.shape),
        ],
        out_specs=[pl.BlockSpec((D_MODEL, tile), lambda i: (0, i)), rows, rows, tiles, tiles],
        out_shape=[jax.ShapeDtypeStruct((D_MODEL, n_tok), BF16), rows_shape, rows_shape, tiles_shape, tiles_shape],
        scratch_shapes=[pltpu.VMEM((2 * PEER_HEADS, PEER_KEYS, tile), F32),
                        pltpu.VMEM((3, 24, tile), F32), pltpu.VMEM((1, CAND_ROWS, tile), F32)],
        compiler_params=_params(("parallel",)),
        name="peer_router",
    )(h, norm_g.reshape(1, -1), qwh, qwl, skh, skl)


def _rows_as_bf16(words):
    return pltpu.bitcast(jnp.broadcast_to(words, (PEER_KEYS // 2, words.shape[1])), BF16)


def _dense_kernel(h_ref, hnT_ref, cnt_ref, e1_ref, rank_ref, e2_ref, u_ref, u_next_ref, vT_ref, vT_prev_ref, gf_ref,
                  o_ref, acc_ref, act_ref, p_ref, *, final_norm):
    j = pl.program_id(1)
    eblk, tile = u_ref.shape[0], hnT_ref.shape[1]
    n_sub = eblk // DENSE_SUB
    hnT = hnT_ref[...]

    @pl.when(j == 0)
    def _():
        acc_ref[...] = jnp.zeros(acc_ref.shape, F32)
        p_ref[1] = jnp.zeros(p_ref.shape[1:], BF16)
        act_ref[0] = _dot(u_ref[0:DENSE_SUB, :], hnT)

    for sb in range(n_sub):
        r0 = sb * DENSE_SUB
        cur, nxt = sb % 2, (sb + 1) % 2
        u_rows = u_ref[r0 + DENSE_SUB:r0 + 2 * DENSE_SUB, :] if sb + 1 < n_sub else u_next_ref[...]
        v_cols = vT_ref[:, r0 - DENSE_SUB:r0] if sb > 0 else vT_prev_ref[...]
        for half in range(tile // 256):
            hc = slice(half * 256, (half + 1) * 256)
            act_ref[nxt, :, hc] = _dot(u_rows, hnT_ref[:, hc])
            acc_ref[:, hc] += _dot(v_cols, p_ref[nxt, :, hc])
        als = range(r0 // PEER_KEYS, (r0 + DENSE_SUB) // PEER_KEYS)
        for tc in range(tile // 128):
            cols = slice(tc * 128, (tc + 1) * 128)
            gates = [jnp.zeros((PEER_KEYS, 128), BF16) for _ in als]
            for h in range(PEER_HEADS):
                rank = pltpu.bitcast(rank_ref[h, tc], BF16)
                e2 = pltpu.bitcast(e2_ref[h, tc], BF16)
                for i, al in enumerate(als):
                    cnt_row = _rows_as_bf16(cnt_ref[h, al:al + 1, cols])
                    e1_row = _rows_as_bf16(e1_ref[h, al:al + 1, cols])
                    gates[i] = gates[i] + jnp.where(rank < cnt_row, e2 * e1_row, jnp.zeros((), BF16))
            for i, al in enumerate(als):
                rows = slice(al * PEER_KEYS - r0, (al + 1) * PEER_KEYS - r0)
                x = act_ref[cur, rows, cols]
                gelu = 0.5 * x * (1.0 + lax.erf(x * INV_SQRT2))
                p_ref[cur, rows, cols] = gates[i] * gelu.astype(BF16)

    @pl.when(j == pl.num_programs(1) - 1)
    def _():
        acc = acc_ref[...] + _dot(vT_ref[:, eblk - DENSE_SUB:eblk], p_ref[(n_sub - 1) % 2])
        out = h_ref[...] + acc.T
        if final_norm:
            out = _rmsnorm(out, gf_ref[...])
        o_ref[...] = out


def _peer_dense(h, hnT, cnt, e1, rank, e2, u_bf, vT_bf, final_g, final_norm):
    n_tok = h.shape[0]
    tile, eblk = DENSE_TILE, DENSE_EBLK
    n_sub = eblk // DENSE_SUB
    last_sub = PEER_EXPERTS // DENSE_SUB - 1
    assert n_sub % 2 == 0
    rows = pl.BlockSpec((PEER_HEADS, eblk // PEER_KEYS, tile), lambda i, j: (0, j, i))
    full = pl.BlockSpec((PEER_HEADS, tile // 128, PEER_KEYS // 2, 128), lambda i, j: (0, i, 0, 0))
    return pl.pallas_call(
        functools.partial(_dense_kernel, final_norm=final_norm),
        grid=(n_tok // tile, PEER_EXPERTS // eblk),
        in_specs=[
            pl.BlockSpec((tile, D_MODEL), lambda i, j: (i, 0)),
            pl.BlockSpec((D_MODEL, tile), lambda i, j: (0, i)),
            rows, rows, full, full,
            pl.BlockSpec((eblk, D_MODEL), lambda i, j: (j, 0)),
            pl.BlockSpec((DENSE_SUB, D_MODEL), lambda i, j: (jnp.minimum((j + 1) * n_sub, last_sub), 0)),
            pl.BlockSpec((D_MODEL, eblk), lambda i, j: (0, j)),
            pl.BlockSpec((D_MODEL, DENSE_SUB), lambda i, j: (0, jnp.maximum(j * n_sub - 1, 0))),
            _const_spec((1, D_MODEL)),
        ],
        out_specs=pl.BlockSpec((tile, D_MODEL), lambda i, j: (i, 0)),
        out_shape=jax.ShapeDtypeStruct((n_tok, D_MODEL), F32),
        scratch_shapes=[
            pltpu.VMEM((D_MODEL, tile), F32),
            pltpu.VMEM((2, DENSE_SUB, tile), F32),
            pltpu.VMEM((2, DENSE_SUB, tile), BF16),
        ],
        compiler_params=_params(("parallel", "arbitrary")),
        name="peer_dense",
    )(h, hnT, cnt, e1, rank, e2, u_bf, u_bf, vT_bf, vT_bf, final_g.reshape(1, -1))


def _peer_layer(h, norm_g, q_w, subkeys, u_tab, v_tab, final_g, final_norm):
    hnT, cnt, e1, rank, e2 = _peer_router(h, norm_g, q_w, subkeys)
    return _peer_dense(h, hnT, cnt, e1, rank, e2, u_tab.astype(BF16), v_tab.T.astype(BF16), final_g, final_norm)


def _head_sum(x, ind_ref, indT_ref):
    return _dot2(_dot2(x, ind_ref[...]), indT_ref[...])


def _proj_kernel(flags_ref, x_ref, xp_ref, xn_ref, g_ref, mu_ref, w_ref,
                 dw0_ref, dw1_ref, dw2_ref, a0_ref, a1_ref, a2_ref, g1_ref, g2_ref,
                 kk_ref_w, ka_ref, ind_ref, indT_ref,
                 r_ref, k_ref, v_ref, gate_ref, kko_ref, lw_ref, kd_ref, beta_ref, xe_ref):
    i = pl.program_id(0)
    tile = x_ref.shape[0]
    first = flags_ref[0, i]
    last = flags_ref[1, i]
    g = g_ref[...]
    u = _rmsnorm(x_ref[...], g)
    _fill_extended(xe_ref, u, _rmsnorm(xp_ref[...], g), _rmsnorm(xn_ref[...], g), first, last, tile)
    xx = 0.5 * (xe_ref[pl.ds(HALO - 1, tile), :] + xe_ref[pl.ds(HALO + 1, tile), :]) - u
    xr, xw, xk, xv, xa, xg = (u + xx * mu_ref[m:m + 1, :] for m in range(6))
    r = _dot(xr.astype(BF16), w_ref[0])
    k = _dot(xk.astype(BF16), w_ref[1])
    v = _dot(xv.astype(BF16), w_ref[2])
    r_ref[...] = r
    k_ref[...] = k
    v_ref[...] = v
    gate_ref[...] = _dot(jax.nn.sigmoid(_dot(xg.astype(BF16), g1_ref[...])).astype(BF16), g2_ref[...])
    kk = k * kk_ref_w[...]
    nrm = jnp.sqrt(_head_sum(kk * kk, ind_ref, indT_ref))
    kk = kk / jnp.maximum(nrm, 1e-12)
    kko_ref[...] = kk
    lora = DECAY_LORA
    tw = jnp.tanh(_dot(xw.astype(BF16), dw1_ref[...])).astype(BF16)
    aw = _dot(xa.astype(BF16), a1_ref[...]).astype(BF16)
    for d in range(2):
        z = dw0_ref[d:d + 1, :] + _dot(tw[:, d * lora:(d + 1) * lora], dw2_ref[d])
        lw_ref[d] = -DECAY_SCALE * jax.nn.sigmoid(z)
        a = jax.nn.sigmoid(a0_ref[d:d + 1, :] + _dot(aw[:, d * lora:(d + 1) * lora], a2_ref[d]))
        kd_ref[d] = k * (1.0 + (a - 1.0) * ka_ref[...])
        beta_ref[d] = kk * a


def _head_indicator():
    ind = (np.arange(D_MODEL)[:, None] // RWKV_HEAD == np.arange(RWKV_HEADS)[None, :]).astype(np.float32)
    pad = np.zeros((D_MODEL, 128), np.float32)
    pad[:, :RWKV_HEADS] = ind
    return jnp.asarray(pad, BF16), jnp.asarray(pad.T, BF16)


def _rwkv_proj(h, flags, norm_g, mu, w_rkv, dec_w0, dec_w1, dec_w2, icl_a0, icl_a1, icl_a2,
               gate_g1, gate_g2, k_k, k_a):
    n_tok = h.shape[0]
    tile = PROJ_TILE
    ind, indT = _head_indicator()
    assert dec_w1.shape[-1] == DECAY_LORA == icl_a1.shape[-1]
    side_by_side = lambda w: jnp.concatenate([w[0], w[1]], axis=1).astype(BF16)
    consts = [norm_g.reshape(1, -1), mu, w_rkv.astype(BF16), dec_w0, side_by_side(dec_w1), dec_w2.astype(BF16),
              icl_a0, side_by_side(icl_a1), icl_a2.astype(BF16), gate_g1.astype(BF16), gate_g2.astype(BF16),
              k_k.reshape(1, -1), k_a.reshape(1, -1), ind, indT]

    tok = pl.BlockSpec((tile, D_MODEL), lambda i, f: (i, 0))
    tok2 = pl.BlockSpec((2, tile, D_MODEL), lambda i, f: (0, i, 0))
    one = jax.ShapeDtypeStruct((n_tok, D_MODEL), F32)
    two = jax.ShapeDtypeStruct((2, n_tok, D_MODEL), F32)
    grid_spec = pltpu.PrefetchScalarGridSpec(
        num_scalar_prefetch=1,
        grid=(n_tok // tile,),
        in_specs=_halo_specs(tile, n_tok) + [_const_spec(a.shape) for a in consts],
        out_specs=[tok] * 5 + [tok2] * 3,
        scratch_shapes=[pltpu.VMEM((tile + 2 * HALO, D_MODEL), F32)],
    )
    return pl.pallas_call(
        _proj_kernel,
        grid_spec=grid_spec,
        out_shape=[one] * 5 + [two] * 3,
        compiler_params=_params(("parallel",)),
        name="rwkv_proj",
    )(flags, h, h, h, *consts)


def _scan_prep(rev, r_ref, lw_ref, kd_ref, kk_ref, beta_ref, incl):
    L = r_ref.shape[0]
    lw = lw_ref[...]
    lw_hi = lw.astype(BF16)
    lw_r = lw - lw_hi.astype(F32)
    lw_mid = lw_r.astype(BF16)
    lw_lo = (lw_r - lw_mid.astype(F32)).astype(BF16)
    inc_bf = incl.astype(BF16)
    cs = _dot(inc_bf, lw_hi) + (_dot(inc_bf, lw_mid) + _dot(inc_bf, lw_lo))
    end = 0 if rev else L - 1
    cs_end = cs[end:end + 1, :]
    g_inv = jnp.exp(-cs)
    g_rest = jnp.exp(cs_end - cs)
    beta = beta_ref[...]
    kd = kd_ref[...]
    return dict(
        alpha_b=-kk_ref[...] * jnp.exp(cs - lw), r_b=r_ref[...] * jnp.exp(cs),
        beta_t=beta * g_inv, k_t=kd * g_inv, beta_h=beta * g_rest, k_h=kd * g_rest,
        g_end=jnp.exp(cs_end))


def _scan_kernel(flags_ref, rf, lwf, kdf, vf, kkf, bf, rb, lwb, kdb, vb, kkb, bb, yf_ref, yb_ref, stf_ref, stb_ref):
    c = pl.program_id(0)
    n = pl.num_programs(0)
    L = rf.shape[0]
    N = RWKV_HEAD
    row = lax.broadcasted_iota(jnp.int32, (L, L), 0)
    col = lax.broadcasted_iota(jnp.int32, (L, L), 1)
    eye = (row == col).astype(F32)
    incl = (row >= col, row <= col)
    strict = (row > col, row < col)
    prep = (_scan_prep(False, rf, lwf, kdf, kkf, bf, incl[0]), _scan_prep(True, rb, lwb, kdb, kkb, bb, incl[1]))
    v_all = (vf[...], vb[...])
    reset = (flags_ref[0, c], flags_ref[1, n - 1 - c])
    st_refs = (stf_ref, stb_ref)
    y_refs = (yf_ref, yb_ref)
    chains = [(d, h) for h in range(RWKV_HEADS) for d in range(2)]

    def cols(d, h, name):
        return prep[d][name][:, h * N:(h + 1) * N]

    t0 = [jnp.where(reset[d] != 0, 0.0, st_refs[d][h]) for d, h in chains]
    x2 = [jnp.concatenate([cols(d, h, "alpha_b"), cols(d, h, "r_b")], axis=0) for d, h in chains]
    y2 = [jnp.concatenate([cols(d, h, "beta_t"), cols(d, h, "k_t")], axis=0) for d, h in chains]
    aa = [_dot1f(a, b, NT) for a, b in zip(x2, y2)]
    xt = [_dot1f(a, b) for a, b in zip(x2, t0)]
    a_ab = [jnp.where(strict[d], m[0:L, 0:L], 0.0) for (d, h), m in zip(chains, aa)]
    a_ak = [jnp.where(strict[d], m[0:L, L:2 * L], 0.0) for (d, h), m in zip(chains, aa)]
    a_r = [jnp.concatenate([jnp.where(incl[d], m[L:2 * L, 0:L], 0.0), jnp.where(incl[d], m[L:2 * L, L:2 * L], 0.0)],
                           axis=1) for (d, h), m in zip(chains, aa)]
    vh = [v_all[d][:, h * N:(h + 1) * N] for d, h in chains]
    rhs = [x[0:L] + _dot1f(a, b) for x, a, b in zip(xt, a_ak, vh)]
    minv = [eye + a for a in a_ab]
    ap = a_ab
    for _ in range(int(math.log2(L)) - 1):
        ap = [_dot1f(a, a) for a in ap]
        minv = [m + _dot1f(m, a) for m, a in zip(minv, ap)]
    u = [_dot1f(m, b) for m, b in zip(minv, rhs)]
    uv = [jnp.concatenate([a, b], axis=0) for a, b in zip(u, vh)]
    y = [x[L:2 * L] + _dot1f(a, b) for x, a, b in zip(xt, a_r, uv)]
    bk = [jnp.concatenate([cols(d, h, "beta_h"), cols(d, h, "k_h")], axis=0) for d, h in chains]
    t1 = [_dot1f(a, b, TN) for a, b in zip(bk, uv)]
    for i, (d, h) in enumerate(chains):
        y_refs[d][:, h * N:(h + 1) * N] = y[i]
        st_refs[d][h] = t0[i] * prep[d]["g_end"][:, h * N:(h + 1) * N].T + t1[i]


def _rwkv_scan(flags, r, lw, kd, v, kk, beta):
    n_tok = r.shape[0]
    L = SCAN_CHUNK
    n = n_tok // L
    fwd = pl.BlockSpec((L, D_MODEL), lambda c, f: (c, 0))
    bwd = pl.BlockSpec((L, D_MODEL), lambda c, f: (n - 1 - c, 0))

    def dspec(d, rev):
        return pl.BlockSpec((None, L, D_MODEL), (lambda c, f: (d, n - 1 - c, 0)) if rev else (lambda c, f: (d, c, 0)))

    grid_spec = pltpu.PrefetchScalarGridSpec(
        num_scalar_prefetch=1,
        grid=(n,),
        in_specs=[fwd, dspec(0, False), dspec(0, False), fwd, fwd, dspec(0, False),
                  bwd, dspec(1, True), dspec(1, True), bwd, bwd, dspec(1, True)],
        out_specs=[fwd, bwd],
        scratch_shapes=[pltpu.VMEM((RWKV_HEADS, RWKV_HEAD, RWKV_HEAD), F32)] * 2,
    )
    one = jax.ShapeDtypeStruct((n_tok, D_MODEL), F32)
    return pl.pallas_call(
        _scan_kernel,
        grid_spec=grid_spec,
        out_shape=[one, one],
        compiler_params=_params(("arbitrary",)),
        name="rwkv_scan",
    )(flags, r, lw, kd, v, kk, beta, r, lw, kd, v, kk, beta)


def _post_kernel(h_ref, yf_ref, yb_ref, r_ref, k_ref, v_ref, g_ref, rk_ref, lg_ref, lb_ref,
                 wo_ref, ind_ref, indT_ref, o_ref):
    y = yf_ref[...] + yb_ref[...]
    inv_n = 1.0 / RWKV_HEAD
    mean = _head_sum(y, ind_ref, indT_ref) * inv_n
    yc = y - mean
    var = _head_sum(yc * yc, ind_ref, indT_ref) * inv_n
    yn = yc * lax.rsqrt(var + LNX_EPS) * lg_ref[...] + lb_ref[...]
    bonus = _head_sum(r_ref[...] * k_ref[...] * rk_ref[...], ind_ref, indT_ref) * v_ref[...]
    out = _dot(((yn + bonus) * g_ref[...]).astype(BF16), wo_ref[...])
    o_ref[...] = h_ref[...] + out


def _rwkv_post(h, yf, yb, r, k, v, g, r_k, lnx_g, lnx_b, w_o):
    n_tok = h.shape[0]
    tile = POST_TILE
    ind, indT = _head_indicator()
    tok = pl.BlockSpec((tile, D_MODEL), lambda i: (i, 0))
    consts = [r_k.reshape(1, -1), lnx_g.reshape(1, -1), lnx_b.reshape(1, -1), w_o.astype(BF16), ind, indT]
    return pl.pallas_call(
        _post_kernel,
        grid=(n_tok // tile,),
        in_specs=[tok] * 7 + [_const_spec(a.shape) for a in consts],
        out_specs=tok,
        out_shape=jax.ShapeDtypeStruct((n_tok, D_MODEL), F32),
        compiler_params=_params(("parallel",)),
        name="rwkv_post",
    )(h, yf, yb, r, k, v, g, *consts)


def _rwkv_layer(h, seq_lens, norm_g, mu, w_rkv, w_o, dec_w0, dec_w1, dec_w2, icl_a0, icl_a1, icl_a2,
                gate_g1, gate_g2, k_k, k_a, r_k, lnx_g, lnx_b):
    r, k, v, g, kk, lw, kd, beta = _rwkv_proj(
        h, _tile_flags(seq_lens, PROJ_TILE), norm_g, mu, w_rkv, dec_w0, dec_w1, dec_w2,
        icl_a0, icl_a1, icl_a2, gate_g1, gate_g2, k_k, k_a)
    yf, yb = _rwkv_scan(_tile_flags(seq_lens, SCAN_CHUNK), r, lw, kd, v, kk, beta)
    return _rwkv_post(h, yf, yb, r, k, v, g, r_k, lnx_g, lnx_b, w_o)


def _trunk(h, seq_lens, norm_mix, norm_ffn, norm_final, pool_w, pool_scale,
           rwkv_mu, rwkv_w_rkv, rwkv_w_o, rwkv_dec_w0, rwkv_dec_w1, rwkv_dec_w2,
           rwkv_icl_a0, rwkv_icl_a1, rwkv_icl_a2, rwkv_gate_g1, rwkv_gate_g2,
           rwkv_k_k, rwkv_k_a, rwkv_r_k, rwkv_lnx_g, rwkv_lnx_b,
           peer_q, peer_subkeys, peer_u, peer_v):
    depth = norm_mix.shape[0]
    for i in range(depth):
        j = i // 2
        if i % 2 == 0:
            h = _pool_layer(h, _tile_flags(seq_lens, POOL_TILE), norm_mix[i], pool_w[j], pool_scale[j])
        else:
            h = _rwkv_layer(h, seq_lens, norm_mix[i], rwkv_mu[j], rwkv_w_rkv[j], rwkv_w_o[j],
                            rwkv_dec_w0[j], rwkv_dec_w1[j], rwkv_dec_w2[j], rwkv_icl_a0[j],
                            rwkv_icl_a1[j], rwkv_icl_a2[j], rwkv_gate_g1[j], rwkv_gate_g2[j],
                            rwkv_k_k[j], rwkv_k_a[j], rwkv_r_k[j].reshape(-1), rwkv_lnx_g[j], rwkv_lnx_b[j])
        h = _peer_layer(h, norm_ffn[i], peer_q[i], peer_subkeys[i], peer_u[i], peer_v[i],
                        norm_final, final_norm=(i == depth - 1))
    return h


def kernel(x_prompt, x_sample, norm_mix, norm_ffn, norm_final, pool_w, pool_scale, rwkv_mu, rwkv_w_rkv, rwkv_w_o, rwkv_dec_w0, rwkv_dec_w1, rwkv_dec_w2, rwkv_icl_a0, rwkv_icl_a1, rwkv_icl_a2, rwkv_gate_g1, rwkv_gate_g2, rwkv_k_k, rwkv_k_a, rwkv_r_k, rwkv_lnx_g, rwkv_lnx_b, peer_q, peer_subkeys, peer_u, peer_v):
    bp, sp, d = x_prompt.shape
    bs, ss, _ = x_sample.shape
    seq_lens = (sp,) * bp + (ss,) * bs
    h = jnp.concatenate([x_prompt.reshape(-1, d), x_sample.reshape(-1, d)], axis=0)
    y = _trunk(h, seq_lens, norm_mix, norm_ffn, norm_final, pool_w, pool_scale,
               rwkv_mu, rwkv_w_rkv, rwkv_w_o, rwkv_dec_w0, rwkv_dec_w1, rwkv_dec_w2,
               rwkv_icl_a0, rwkv_icl_a1, rwkv_icl_a2, rwkv_gate_g1, rwkv_gate_g2,
               rwkv_k_k, rwkv_k_a, rwkv_r_k, rwkv_lnx_g, rwkv_lnx_b,
               peer_q, peer_subkeys, peer_u, peer_v)
    n_p = bp * sp
    return (y[:n_p].reshape(bp, sp, d), y[n_p:].reshape(bs, ss, d))
```

```python
import functools
import math

import numpy as np
import jax
import jax.numpy as jnp
from jax import lax
from jax.experimental import pallas as pl
from jax.experimental.pallas import tpu as pltpu

F32 = jnp.float32
BF16 = jnp.bfloat16

D_MODEL = 1024
NORM_EPS = 1e-6
POOL_WINDOWS = (2, 4, 8, 16)
POOL_GROUP = D_MODEL // len(POOL_WINDOWS)
RWKV_HEAD = 64
RWKV_HEADS = D_MODEL // RWKV_HEAD
DECAY_SCALE = math.exp(-0.5)
LNX_EPS = 64e-5
DECAY_LORA = 64
PEER_HEADS = 8
PEER_KEYS = 128
PEER_EXPERTS = PEER_KEYS * PEER_KEYS
PEER_TOPK = 16
INV_SQRT2 = 0.7071067811865476

HALO = 8
VMEM_LIMIT = 56 * 1024 * 1024

POOL_TILE = 512
PROJ_TILE = 256
POST_TILE = 256
ROUTER_TILE = 256
DENSE_TILE = 1024
DENSE_EBLK = 1024
DENSE_SUB = 256
SCAN_CHUNK = 64
NTOP = PEER_TOPK + 1
CAND_COUNTS = tuple(min(NTOP, NTOP // (i + 1)) for i in range(NTOP))
CAND_ROWS = -(-sum(CAND_COUNTS) // 8) * 8


def _params(sem):
    return pltpu.CompilerParams(dimension_semantics=sem, vmem_limit_bytes=VMEM_LIMIT)


def _split(w):
    hi = w.astype(BF16)
    lo = (w - hi.astype(F32)).astype(BF16)
    return hi, lo


def _dot(a, b, dims=(((1,), (0,)), ((), ()))):
    return lax.dot_general(a, b, dims, preferred_element_type=F32)


def _dot3(a, b_hi, b_lo, dims=(((1,), (0,)), ((), ()))):
    a_hi, a_lo = _split(a)
    return _dot(a_hi, b_hi, dims) + (_dot(a_lo, b_hi, dims) + _dot(a_hi, b_lo, dims))


def _dot2(a, b_exact, dims=(((1,), (0,)), ((), ()))):
    a_hi, a_lo = _split(a)
    return _dot(a_hi, b_exact, dims) + _dot(a_lo, b_exact, dims)


def _dot3f(a, b, dims=(((1,), (0,)), ((), ()))):
    b_hi, b_lo = _split(b)
    return _dot3(a, b_hi, b_lo, dims)


def _dot1f(a, b, dims=(((1,), (0,)), ((), ()))):
    return _dot(a.astype(BF16), b.astype(BF16), dims)


NT = (((1,), (1,)), ((), ()))
TN = (((0,), (0,)), ((), ()))


def _rmsnorm(x, g):
    return x * lax.rsqrt(jnp.mean(x * x, axis=-1, keepdims=True) + NORM_EPS) * g


def _tile_flags(seq_lens, tile):
    first, last = [], []
    for s in seq_lens:
        n = s // tile
        assert n * tile == s
        first += [1] + [0] * (n - 1)
        last += [0] * (n - 1) + [1]
    return jnp.asarray(np.array([first, last], np.int32))


def _halo_specs(tile, n_tok):
    r = tile // HALO
    nb = n_tok // HALO
    return [
        pl.BlockSpec((tile, D_MODEL), lambda i, f: (i, 0)),
        pl.BlockSpec((HALO, D_MODEL), lambda i, f: (jnp.maximum(i * r - 1, 0), 0)),
        pl.BlockSpec((HALO, D_MODEL), lambda i, f: (jnp.minimum((i + 1) * r, nb - 1), 0)),
    ]


def _fill_extended(xe_ref, u, up, un, first, last, tile):
    xe_ref[0:HALO, :] = jnp.where(first != 0, 0.0, up)
    xe_ref[HALO:HALO + tile, :] = u
    xe_ref[HALO + tile:2 * HALO + tile, :] = jnp.where(last != 0, 0.0, un)


def _const_spec(shape):
    nd = len(shape)
    return pl.BlockSpec(shape, lambda *_: (0,) * nd, pipeline_mode=pl.Buffered(1))


def _pool_kernel(flags_ref, x_ref, xp_ref, xn_ref, g_ref, pwh_ref, pwl_ref, ps_ref, o_ref, xe_ref):
    i = pl.program_id(0)
    tile = x_ref.shape[0]
    first = flags_ref[0, i]
    last = flags_ref[1, i]
    g = g_ref[...]
    x = x_ref[...]
    u = _rmsnorm(x, g)
    _fill_extended(xe_ref, u, _rmsnorm(xp_ref[...], g), _rmsnorm(xn_ref[...], g), first, last, tile)
    row = lax.broadcasted_iota(jnp.int32, (tile, 1), 0)
    for gi, w in enumerate(POOL_WINDOWS):
        sl = slice(gi * POOL_GROUP, (gi + 1) * POOL_GROUP)
        acc = xe_ref[pl.ds(HALO - w // 2, tile), sl]
        for j in range(-w // 2 + 1, w // 2):
            acc = acc + xe_ref[pl.ds(HALO + j, tile), sl]
        lo_clip = jnp.where(first != 0, jnp.maximum(w // 2 - row, 0), 0)
        hi_clip = jnp.where(last != 0, jnp.maximum(row + w // 2 - tile, 0), 0)
        cnt = (w - lo_clip - hi_clip).astype(F32)
        diff = acc / cnt - u[:, sl]
        y = _dot3(diff, pwh_ref[gi], pwl_ref[gi])
        o_ref[:, sl] = x[:, sl] + y * ps_ref[:, sl]


def _pool_layer(h, flags, norm_g, pool_w, pool_scale):
    n_tok = h.shape[0]
    tile = POOL_TILE
    pwh, pwl = _split(pool_w)
    grid_spec = pltpu.PrefetchScalarGridSpec(
        num_scalar_prefetch=1,
        grid=(n_tok // tile,),
        in_specs=_halo_specs(tile, n_tok) + [
            _const_spec((1, D_MODEL)), _const_spec(pwh.shape), _const_spec(pwl.shape),
            _const_spec((1, D_MODEL)),
        ],
        out_specs=pl.BlockSpec((tile, D_MODEL), lambda i, f: (i, 0)),
        scratch_shapes=[pltpu.VMEM((tile + 2 * HALO, D_MODEL), F32)],
    )
    return pl.pallas_call(
        _pool_kernel,
        grid_spec=grid_spec,
        out_shape=jax.ShapeDtypeStruct((n_tok, D_MODEL), F32),
        compiler_params=_params(("parallel",)),
        name="pool",
    )(flags, h, h, h, norm_g.reshape(1, -1), pwh, pwl, pool_scale.reshape(1, -1))


def _top_values(cur, n, out_ref, slot):
    for r in range(n):
        mx = jnp.max(cur, axis=0, keepdims=True)
        out_ref[slot, r:r + 1, :] = mx
        cur = jnp.where(cur == mx, -jnp.inf, cur)


def _sort_network(n):
    pairs = []

    def merge(lo, hi, r):
        step = r * 2
        if step < hi - lo:
            merge(lo, hi, step)
            merge(lo + r, hi, step)
            pairs.extend((i, i + r) for i in range(lo + r, hi - r, step))
        else:
            pairs.append((lo, lo + r))

    def sort(lo, hi):
        if hi - lo >= 1:
            mid = lo + (hi - lo) // 2
            sort(lo, mid)
            sort(mid + 1, hi)
            merge(lo, hi, 1)

    sort(0, n - 1)
    return pairs


def _top_values_sorted(s, n, out_ref, slot):
    groups = PEER_KEYS // 8
    v = [s[8 * k:8 * (k + 1), :] for k in range(groups)]
    for i, j in _sort_network(groups):
        v[i], v[j] = jnp.maximum(v[i], v[j]), jnp.minimum(v[i], v[j])
    for r in range(n):
        mx = jnp.max(v[0], axis=0, keepdims=True)
        out_ref[slot, r:r + 1, :] = mx
        hit = v[0] == mx
        for k in range(min(groups, n - r - 1)):
            v[k] = jnp.where(hit, v[k + 1] if k + 1 < groups else -jnp.inf, v[k])


def _bf16_pair_words(x):
    bits = pltpu.bitcast(x.astype(BF16).astype(F32), jnp.uint32)
    return bits | lax.shift_right_logical(bits, jnp.uint32(16))


def _router_kernel(h_ref, g_ref, qwh_ref, qwl_ref, skh_ref, skl_ref,
                   hnT_ref, cnt_ref, e1_ref, rank_ref, e2_ref, s_ref, tops_ref, cand_ref):
    hn = _rmsnorm(h_ref[...], g_ref[...])
    hnT_ref[...] = hn.T.astype(BF16)
    q = _dot3(hn, qwh_ref[...], qwl_ref[...])
    for hp in range(2 * PEER_HEADS):
        qs = q[:, hp * PEER_KEYS:(hp + 1) * PEER_KEYS]
        qh, ql = _split(qs)
        skh = skh_ref[hp]
        s_ref[hp] = _dot(skh, qh, NT) + (_dot(skl_ref[hp], qh, NT) + _dot(skh, ql, NT))
    for h in range(PEER_HEADS):
        s1 = s_ref[2 * h]
        s2 = s_ref[2 * h + 1]
        _top_values_sorted(s1, NTOP, tops_ref, 0)
        _top_values_sorted(s2, NTOP, tops_ref, 1)
        cand_ref[...] = jnp.full(cand_ref.shape, -jnp.inf, F32)
        off = 0
        for i, n in enumerate(CAND_COUNTS):
            cand_ref[0, off:off + n, :] = tops_ref[0, i:i + 1, :] + tops_ref[1, 0:n, :]
            off += n
        m1 = tops_ref[0, 0:1, :]
        m2 = tops_ref[1, 0:1, :]
        _top_values(cand_ref[0], NTOP, tops_ref, 2)
        top = tops_ref[2, 0:PEER_TOPK, :]
        z = jnp.sum(jnp.exp(top - top[0:1, :]), axis=0, keepdims=True)
        tau = 0.5 * (tops_ref[2, PEER_TOPK - 1:PEER_TOPK, :] + tops_ref[2, PEER_TOPK:PEER_TOPK + 1, :])
        thr = tau - s1
        cnt = jnp.zeros_like(s1)
        rank = jnp.zeros_like(s2)
        for k in range(NTOP):
            tk = tops_ref[1, k:k + 1, :]
            cnt = jnp.where(tk >= thr, float(k + 1), cnt)
            rank = jnp.where(tk > s2, float(k + 1), rank)
        cnt_ref[h] = _bf16_pair_words(cnt)
        e1_ref[h] = _bf16_pair_words(jnp.exp(s1 - m1) * (1.0 / z))
        e2 = jnp.exp(s2 - m2)
        for tc in range(rank_ref.shape[1]):
            cols = slice(tc * 128, (tc + 1) * 128)
            rank_ref[h, tc] = pltpu.bitcast(rank[:, cols].astype(BF16), jnp.uint32)
            e2_ref[h, tc] = pltpu.bitcast(e2[:, cols].astype(BF16), jnp.uint32)


def _peer_router(h, norm_g, q_w, subkeys):
    n_tok = h.shape[0]
    tile = ROUTER_TILE
    qwh, qwl = _split(q_w)
    sk = subkeys.reshape(2 * PEER_HEADS, PEER_KEYS, PEER_KEYS)
    skh, skl = _split(sk)
    rows = pl.BlockSpec((PEER_HEADS, PEER_KEYS, tile), lambda i: (0, 0, i))
    rows_shape = jax.ShapeDtypeStruct((PEER_HEADS, PEER_KEYS, n_tok), jnp.uint32)
    tiles = pl.BlockSpec((PEER_HEADS, tile // 128, PEER_KEYS // 2, 128), lambda i: (0, i, 0, 0))
    tiles_shape = jax.ShapeDtypeStruct((PEER_HEADS, n_tok // 128, PEER_KEYS // 2, 128), jnp.uint32)
    return pl.pallas_call(
        _router_kernel,
        grid=(n_tok // tile,),
        in_specs=[
            pl.BlockSpec((tile, D_MODEL), lambda i: (i, 0)),
            _const_spec((1, D_MODEL)),
            _const_spec(qwh.shape), _const_spec(qwl.shape),
            _const_spec(skh.shape), _const_spec(skl.shape),
        ],
        out_specs=[pl.BlockSpec((D_MODEL, tile), lambda i: (0, i)), rows, rows, tiles, tiles],
        out_shape=[jax.ShapeDtypeStruct((D_MODEL, n_tok), BF16), rows_shape, rows_shape, tiles_shape, tiles_shape],
        scratch_shapes=[pltpu.VMEM((2 * PEER_HEADS, PEER_KEYS, tile), F32),
                        pltpu.VMEM((3, 24, tile), F32), pltpu.VMEM((1, CAND_ROWS, tile), F32)],
        compiler_params=_params(("parallel",)),
        name="peer_router",
    )(h, norm_g.reshape(1, -1), qwh, qwl, skh, skl)


def _rows_as_bf16(words):
    return pltpu.bitcast(jnp.broadcast_to(words, (PEER_KEYS // 2, words.shape[1])), BF16)


def _dense_kernel(h_ref, hnT_ref, cnt_ref, e1_ref, rank_ref, e2_ref, u_ref, u_next_ref, vT_ref, vT_prev_ref, gf_ref,
                  o_ref, acc_ref, act_ref, p_ref, *, final_norm):
    j = pl.program_id(1)
    eblk, tile = u_ref.shape[0], hnT_ref.shape[1]
    n_sub = eblk // DENSE_SUB
    hnT = hnT_ref[...]

    @pl.when(j == 0)
    def _():
        acc_ref[...] = jnp.zeros(acc_ref.shape, F32)
        p_ref[1] = jnp.zeros(p_ref.shape[1:], BF16)
        act_ref[0] = _dot(u_ref[0:DENSE_SUB, :], hnT)

    for sb in range(n_sub):
        r0 = sb * DENSE_SUB
        cur, nxt = sb % 2, (sb + 1) % 2
        u_rows = u_ref[r0 + DENSE_SUB:r0 + 2 * DENSE_SUB, :] if sb + 1 < n_sub else u_next_ref[...]
        v_cols = vT_ref[sb - 1] if sb > 0 else vT_prev_ref[0]
        for half in range(tile // 256):
            hc = slice(half * 256, (half + 1) * 256)
            act_ref[nxt, :, hc] = _dot(u_rows, hnT_ref[:, hc])
            acc_ref[:, hc] += _dot(v_cols, p_ref[nxt, :, hc])
        als = range(r0 // PEER_KEYS, (r0 + DENSE_SUB) // PEER_KEYS)
        for tc in range(tile // 128):
            cols = slice(tc * 128, (tc + 1) * 128)
            gates = [jnp.zeros((PEER_KEYS, 128), BF16) for _ in als]
            for h in range(PEER_HEADS):
                rank = pltpu.bitcast(rank_ref[h, tc], BF16)
                e2 = pltpu.bitcast(e2_ref[h, tc], BF16)
                for i, al in enumerate(als):
                    cnt_row = _rows_as_bf16(cnt_ref[h, al:al + 1, cols])
                    e1_row = _rows_as_bf16(e1_ref[h, al:al + 1, cols])
                    gates[i] = gates[i] + jnp.where(rank < cnt_row, e2 * e1_row, jnp.zeros((), BF16))
            for i, al in enumerate(als):
                rows = slice(al * PEER_KEYS - r0, (al + 1) * PEER_KEYS - r0)
                x = act_ref[cur, rows, cols]
                gelu = 0.5 * x * (1.0 + lax.erf(x * INV_SQRT2))
                p_ref[cur, rows, cols] = gates[i] * gelu.astype(BF16)

    @pl.when(j == pl.num_programs(1) - 1)
    def _():
        acc = acc_ref[...] + _dot(vT_ref[n_sub - 1], p_ref[(n_sub - 1) % 2])
        out = h_ref[...] + acc.T
        if final_norm:
            out = _rmsnorm(out, gf_ref[...])
        o_ref[...] = out


def _peer_dense(h, hnT, cnt, e1, rank, e2, u_bf, vT_bf, final_g, final_norm):
    n_tok = h.shape[0]
    tile, eblk = DENSE_TILE, DENSE_EBLK
    n_sub = eblk // DENSE_SUB
    last_sub = PEER_EXPERTS // DENSE_SUB - 1
    assert n_sub % 2 == 0
    rows = pl.BlockSpec((PEER_HEADS, eblk // PEER_KEYS, tile), lambda i, j: (0, j, i))
    full = pl.BlockSpec((PEER_HEADS, tile // 128, PEER_KEYS // 2, 128), lambda i, j: (0, i, 0, 0))
    return pl.pallas_call(
        functools.partial(_dense_kernel, final_norm=final_norm),
        grid=(n_tok // tile, PEER_EXPERTS // eblk),
        in_specs=[
            pl.BlockSpec((tile, D_MODEL), lambda i, j: (i, 0)),
            pl.BlockSpec((D_MODEL, tile), lambda i, j: (0, i)),
            rows, rows, full, full,
            pl.BlockSpec((eblk, D_MODEL), lambda i, j: (j, 0)),
            pl.BlockSpec((DENSE_SUB, D_MODEL), lambda i, j: (jnp.minimum((j + 1) * n_sub, last_sub), 0)),
            pl.BlockSpec((n_sub, D_MODEL, DENSE_SUB), lambda i, j: (j, 0, 0)),
            pl.BlockSpec((1, D_MODEL, DENSE_SUB), lambda i, j: (jnp.maximum(j * n_sub - 1, 0), 0, 0)),
            _const_spec((1, D_MODEL)),
        ],
        out_specs=pl.BlockSpec((tile, D_MODEL), lambda i, j: (i, 0)),
        out_shape=jax.ShapeDtypeStruct((n_tok, D_MODEL), F32),
        scratch_shapes=[
            pltpu.VMEM((D_MODEL, tile), F32),
            pltpu.VMEM((2, DENSE_SUB, tile), F32),
            pltpu.VMEM((2, DENSE_SUB, tile), BF16),
        ],
        compiler_params=_params(("parallel", "arbitrary")),
        name="peer_dense",
    )(h, hnT, cnt, e1, rank, e2, u_bf, u_bf, vT_bf, vT_bf, final_g.reshape(1, -1))


def _peer_layer(h, norm_g, q_w, subkeys, u_tab, v_tab, final_g, final_norm):
    hnT, cnt, e1, rank, e2 = _peer_router(h, norm_g, q_w, subkeys)
    vT = v_tab.reshape(PEER_EXPERTS // DENSE_SUB, DENSE_SUB, D_MODEL).transpose(0, 2, 1).astype(BF16)
    return _peer_dense(h, hnT, cnt, e1, rank, e2, u_tab.astype(BF16), vT, final_g, final_norm)


def _head_sum(x, ind_ref, indT_ref):
    return _dot2(_dot2(x, ind_ref[...]), indT_ref[...])


def _proj_kernel(flags_ref, x_ref, xp_ref, xn_ref, g_ref, mu_ref, w_ref,
                 dw0_ref, dw1_ref, dw2_ref, a0_ref, a1_ref, a2_ref, g1_ref, g2_ref,
                 kk_ref_w, ka_ref, ind_ref, indT_ref,
                 r_ref, k_ref, v_ref, gate_ref, kko_ref, lw_ref, kd_ref, beta_ref, xe_ref):
    i = pl.program_id(0)
    tile = x_ref.shape[0]
    first = flags_ref[0, i]
    last = flags_ref[1, i]
    g = g_ref[...]
    u = _rmsnorm(x_ref[...], g)
    _fill_extended(xe_ref, u, _rmsnorm(xp_ref[...], g), _rmsnorm(xn_ref[...], g), first, last, tile)
    xx = 0.5 * (xe_ref[pl.ds(HALO - 1, tile), :] + xe_ref[pl.ds(HALO + 1, tile), :]) - u
    xr, xw, xk, xv, xa, xg = (u + xx * mu_ref[m:m + 1, :] for m in range(6))
    r = _dot(xr.astype(BF16), w_ref[0])
    k = _dot(xk.astype(BF16), w_ref[1])
    v = _dot(xv.astype(BF16), w_ref[2])
    r_ref[...] = r
    k_ref[...] = k
    v_ref[...] = v
    gate_ref[...] = _dot(jax.nn.sigmoid(_dot(xg.astype(BF16), g1_ref[...])).astype(BF16), g2_ref[...])
    kk = k * kk_ref_w[...]
    nrm = jnp.sqrt(_head_sum(kk * kk, ind_ref, indT_ref))
    kk = kk / jnp.maximum(nrm, 1e-12)
    kko_ref[...] = kk
    lora = DECAY_LORA
    tw = jnp.tanh(_dot(xw.astype(BF16), dw1_ref[...])).astype(BF16)
    aw = _dot(xa.astype(BF16), a1_ref[...]).astype(BF16)
    for d in range(2):
        z = dw0_ref[d:d + 1, :] + _dot(tw[:, d * lora:(d + 1) * lora], dw2_ref[d])
        lw_ref[d] = -DECAY_SCALE * jax.nn.sigmoid(z)
        a = jax.nn.sigmoid(a0_ref[d:d + 1, :] + _dot(aw[:, d * lora:(d + 1) * lora], a2_ref[d]))
        kd_ref[d] = k * (1.0 + (a - 1.0) * ka_ref[...])
        beta_ref[d] = kk * a


def _head_indicator():
    ind = (np.arange(D_MODEL)[:, None] // RWKV_HEAD == np.arange(RWKV_HEADS)[None, :]).astype(np.float32)
    pad = np.zeros((D_MODEL, 128), np.float32)
    pad[:, :RWKV_HEADS] = ind
    return jnp.asarray(pad, BF16), jnp.asarray(pad.T, BF16)


def _rwkv_proj(h, flags, norm_g, mu, w_rkv, dec_w0, dec_w1, dec_w2, icl_a0, icl_a1, icl_a2,
               gate_g1, gate_g2, k_k, k_a):
    n_tok = h.shape[0]
    tile = PROJ_TILE
    ind, indT = _head_indicator()
    assert dec_w1.shape[-1] == DECAY_LORA == icl_a1.shape[-1]
    side_by_side = lambda w: jnp.concatenate([w[0], w[1]], axis=1).astype(BF16)
    consts = [norm_g.reshape(1, -1), mu, w_rkv.astype(BF16), dec_w0, side_by_side(dec_w1), dec_w2.astype(BF16),
              icl_a0, side_by_side(icl_a1), icl_a2.astype(BF16), gate_g1.astype(BF16), gate_g2.astype(BF16),
              k_k.reshape(1, -1), k_a.reshape(1, -1), ind, indT]

    tok = pl.BlockSpec((tile, D_MODEL), lambda i, f: (i, 0))
    tok2 = pl.BlockSpec((2, tile, D_MODEL), lambda i, f: (0, i, 0))
    one = jax.ShapeDtypeStruct((n_tok, D_MODEL), F32)
    two = jax.ShapeDtypeStruct((2, n_tok, D_MODEL), F32)
    grid_spec = pltpu.PrefetchScalarGridSpec(
        num_scalar_prefetch=1,
        grid=(n_tok // tile,),
        in_specs=_halo_specs(tile, n_tok) + [_const_spec(a.shape) for a in consts],
        out_specs=[tok] * 5 + [tok2] * 3,
        scratch_shapes=[pltpu.VMEM((tile + 2 * HALO, D_MODEL), F32)],
    )
    return pl.pallas_call(
        _proj_kernel,
        grid_spec=grid_spec,
        out_shape=[one] * 5 + [two] * 3,
        compiler_params=_params(("parallel",)),
        name="rwkv_proj",
    )(flags, h, h, h, *consts)


def _scan_prep(rev, r_ref, lw_ref, kd_ref, kk_ref, beta_ref, incl):
    L = r_ref.shape[0]
    lw = lw_ref[...]
    lw_hi = lw.astype(BF16)
    lw_r = lw - lw_hi.astype(F32)
    lw_mid = lw_r.astype(BF16)
    lw_lo = (lw_r - lw_mid.astype(F32)).astype(BF16)
    inc_bf = incl.astype(BF16)
    cs = _dot(inc_bf, lw_hi) + (_dot(inc_bf, lw_mid) + _dot(inc_bf, lw_lo))
    end = 0 if rev else L - 1
    cs_end = cs[end:end + 1, :]
    g_inv = jnp.exp(-cs)
    g_rest = jnp.exp(cs_end - cs)
    beta = beta_ref[...]
    kd = kd_ref[...]
    return dict(
        alpha_b=-kk_ref[...] * jnp.exp(cs - lw), r_b=r_ref[...] * jnp.exp(cs),
        beta_t=beta * g_inv, k_t=kd * g_inv, beta_h=beta * g_rest, k_h=kd * g_rest,
        g_end=jnp.exp(cs_end))


def _scan_kernel(flags_ref, rf, lwf, kdf, vf, kkf, bf, rb, lwb, kdb, vb, kkb, bb, yf_ref, yb_ref, stf_ref, stb_ref):
    c = pl.program_id(0)
    n = pl.num_programs(0)
    L = rf.shape[0]
    N = RWKV_HEAD
    row = lax.broadcasted_iota(jnp.int32, (L, L), 0)
    col = lax.broadcasted_iota(jnp.int32, (L, L), 1)
    eye = (row == col).astype(F32)
    incl = (row >= col, row <= col)
    strict = (row > col, row < col)
    prep = (_scan_prep(False, rf, lwf, kdf, kkf, bf, incl[0]), _scan_prep(True, rb, lwb, kdb, kkb, bb, incl[1]))
    v_all = (vf[...], vb[...])
    reset = (flags_ref[0, c], flags_ref[1, n - 1 - c])
    st_refs = (stf_ref, stb_ref)
    y_refs = (yf_ref, yb_ref)
    chains = [(d, h) for h in range(RWKV_HEADS) for d in range(2)]

    def cols(d, h, name):
        return prep[d][name][:, h * N:(h + 1) * N]

    t0 = [jnp.where(reset[d] != 0, 0.0, st_refs[d][h]) for d, h in chains]
    x2 = [jnp.concatenate([cols(d, h, "alpha_b"), cols(d, h, "r_b")], axis=0) for d, h in chains]
    y2 = [jnp.concatenate([cols(d, h, "beta_t"), cols(d, h, "k_t")], axis=0) for d, h in chains]
    aa = [_dot1f(a, b, NT) for a, b in zip(x2, y2)]
    xt = [_dot1f(a, b) for a, b in zip(x2, t0)]
    a_ab = [jnp.where(strict[d], m[0:L, 0:L], 0.0) for (d, h), m in zip(chains, aa)]
    a_ak = [jnp.where(strict[d], m[0:L, L:2 * L], 0.0) for (d, h), m in zip(chains, aa)]
    a_r = [jnp.concatenate([jnp.where(incl[d], m[L:2 * L, 0:L], 0.0), jnp.where(incl[d], m[L:2 * L, L:2 * L], 0.0)],
                           axis=1) for (d, h), m in zip(chains, aa)]
    vh = [v_all[d][:, h * N:(h + 1) * N] for d, h in chains]
    rhs = [x[0:L] + _dot1f(a, b) for x, a, b in zip(xt, a_ak, vh)]
    minv = [eye + a for a in a_ab]
    ap = a_ab
    for _ in range(int(math.log2(L)) - 1):
        ap = [_dot1f(a, a) for a in ap]
        minv = [m + _dot1f(m, a) for m, a in zip(minv, ap)]
    u = [_dot1f(m, b) for m, b in zip(minv, rhs)]
    uv = [jnp.concatenate([a, b], axis=0) for a, b in zip(u, vh)]
    y = [x[L:2 * L] + _dot1f(a, b) for x, a, b in zip(xt, a_r, uv)]
    bk = [jnp.concatenate([cols(d, h, "beta_h"), cols(d, h, "k_h")], axis=0) for d, h in chains]
    t1 = [_dot1f(a, b, TN) for a, b in zip(bk, uv)]
    for i, (d, h) in enumerate(chains):
        y_refs[d][:, h * N:(h + 1) * N] = y[i]
        st_refs[d][h] = t0[i] * prep[d]["g_end"][:, h * N:(h + 1) * N].T + t1[i]


def _rwkv_scan(flags, r, lw, kd, v, kk, beta):
    n_tok = r.shape[0]
    L = SCAN_CHUNK
    n = n_tok // L
    fwd = pl.BlockSpec((L, D_MODEL), lambda c, f: (c, 0))
    bwd = pl.BlockSpec((L, D_MODEL), lambda c, f: (n - 1 - c, 0))

    def dspec(d, rev):
        return pl.BlockSpec((None, L, D_MODEL), (lambda c, f: (d, n - 1 - c, 0)) if rev else (lambda c, f: (d, c, 0)))

    grid_spec = pltpu.PrefetchScalarGridSpec(
        num_scalar_prefetch=1,
        grid=(n,),
        in_specs=[fwd, dspec(0, False), dspec(0, False), fwd, fwd, dspec(0, False),
                  bwd, dspec(1, True), dspec(1, True), bwd, bwd, dspec(1, True)],
        out_specs=[fwd, bwd],
        scratch_shapes=[pltpu.VMEM((RWKV_HEADS, RWKV_HEAD, RWKV_HEAD), F32)] * 2,
    )
    one = jax.ShapeDtypeStruct((n_tok, D_MODEL), F32)
    return pl.pallas_call(
        _scan_kernel,
        grid_spec=grid_spec,
        out_shape=[one, one],
        compiler_params=_params(("arbitrary",)),
        name="rwkv_scan",
    )(flags, r, lw, kd, v, kk, beta, r, lw, kd, v, kk, beta)


def _post_kernel(h_ref, yf_ref, yb_ref, r_ref, k_ref, v_ref, g_ref, rk_ref, lg_ref, lb_ref,
                 wo_ref, ind_ref, indT_ref, o_ref):
    y = yf_ref[...] + yb_ref[...]
    inv_n = 1.0 / RWKV_HEAD
    mean = _head_sum(y, ind_ref, indT_ref) * inv_n
    yc = y - mean
    var = _head_sum(yc * yc, ind_ref, indT_ref) * inv_n
    yn = yc * lax.rsqrt(var + LNX_EPS) * lg_ref[...] + lb_ref[...]
    bonus = _head_sum(r_ref[...] * k_ref[...] * rk_ref[...], ind_ref, indT_ref) * v_ref[...]
    out = _dot(((yn + bonus) * g_ref[...]).astype(BF16), wo_ref[...])
    o_ref[...] = h_ref[...] + out


def _rwkv_post(h, yf, yb, r, k, v, g, r_k, lnx_g, lnx_b, w_o):
    n_tok = h.shape[0]
    tile = POST_TILE
    ind, indT = _head_indicator()
    tok = pl.BlockSpec((tile, D_MODEL), lambda i: (i, 0))
    consts = [r_k.reshape(1, -1), lnx_g.reshape(1, -1), lnx_b.reshape(1, -1), w_o.astype(BF16), ind, indT]
    return pl.pallas_call(
        _post_kernel,
        grid=(n_tok // tile,),
        in_specs=[tok] * 7 + [_const_spec(a.shape) for a in consts],
        out_specs=tok,
        out_shape=jax.ShapeDtypeStruct((n_tok, D_MODEL), F32),
        compiler_params=_params(("parallel",)),
        name="rwkv_post",
    )(h, yf, yb, r, k, v, g, *consts)


def _rwkv_layer(h, seq_lens, norm_g, mu, w_rkv, w_o, dec_w0, dec_w1, dec_w2, icl_a0, icl_a1, icl_a2,
                gate_g1, gate_g2, k_k, k_a, r_k, lnx_g, lnx_b):
    r, k, v, g, kk, lw, kd, beta = _rwkv_proj(
        h, _tile_flags(seq_lens, PROJ_TILE), norm_g, mu, w_rkv, dec_w0, dec_w1, dec_w2,
        icl_a0, icl_a1, icl_a2, gate_g1, gate_g2, k_k, k_a)
    yf, yb = _rwkv_scan(_tile_flags(seq_lens, SCAN_CHUNK), r, lw, kd, v, kk, beta)
    return _rwkv_post(h, yf, yb, r, k, v, g, r_k, lnx_g, lnx_b, w_o)


def _trunk(h, seq_lens, norm_mix, norm_ffn, norm_final, pool_w, pool_scale,
           rwkv_mu, rwkv_w_rkv, rwkv_w_o, rwkv_dec_w0, rwkv_dec_w1, rwkv_dec_w2,
           rwkv_icl_a0, rwkv_icl_a1, rwkv_icl_a2, rwkv_gate_g1, rwkv_gate_g2,
           rwkv_k_k, rwkv_k_a, rwkv_r_k, rwkv_lnx_g, rwkv_lnx_b,
           peer_q, peer_subkeys, peer_u, peer_v):
    depth = norm_mix.shape[0]
    for i in range(depth):
        j = i // 2
        if i % 2 == 0:
            h = _pool_layer(h, _tile_flags(seq_lens, POOL_TILE), norm_mix[i], pool_w[j], pool_scale[j])
        else:
            h = _rwkv_layer(h, seq_lens, norm_mix[i], rwkv_mu[j], rwkv_w_rkv[j], rwkv_w_o[j],
                            rwkv_dec_w0[j], rwkv_dec_w1[j], rwkv_dec_w2[j], rwkv_icl_a0[j],
                            rwkv_icl_a1[j], rwkv_icl_a2[j], rwkv_gate_g1[j], rwkv_gate_g2[j],
                            rwkv_k_k[j], rwkv_k_a[j], rwkv_r_k[j].reshape(-1), rwkv_lnx_g[j], rwkv_lnx_b[j])
        h = _peer_layer(h, norm_ffn[i], peer_q[i], peer_subkeys[i], peer_u[i], peer_v[i],
                        norm_final, final_norm=(i == depth - 1))
    return h


def kernel(x_prompt, x_sample, norm_mix, norm_ffn, norm_final, pool_w, pool_scale, rwkv_mu, rwkv_w_rkv, rwkv_w_o, rwkv_dec_w0, rwkv_dec_w1, rwkv_dec_w2, rwkv_icl_a0, rwkv_icl_a1, rwkv_icl_a2, rwkv_gate_g1, rwkv_gate_g2, rwkv_k_k, rwkv_k_a, rwkv_r_k, rwkv_lnx_g, rwkv_lnx_b, peer_q, peer_subkeys, peer_u, peer_v):
    bp, sp, d = x_prompt.shape
    bs, ss, _ = x_sample.shape
    seq_lens = (sp,) * bp + (ss,) * bs
    h = jnp.concatenate([x_prompt.reshape(-1, d), x_sample.reshape(-1, d)], axis=0)
    y = _trunk(h, seq_lens, norm_mix, norm_ffn, norm_final, pool_w, pool_scale,
               rwkv_mu, rwkv_w_rkv, rwkv_w_o, rwkv_dec_w0, rwkv_dec_w1, rwkv_dec_w2,
               rwkv_icl_a0, rwkv_icl_a1, rwkv_icl_a2, rwkv_gate_g1, rwkv_gate_g2,
               rwkv_k_k, rwkv_k_a, rwkv_r_k, rwkv_lnx_g, rwkv_lnx_b,
               peer_q, peer_subkeys, peer_u, peer_v)
    n_p = bp * sp
    return (y[:n_p].reshape(bp, sp, d), y[n_p:].reshape(bs, ss, d))
```

```python
import functools
import math

import numpy as np
import jax
import jax.numpy as jnp
from jax import lax
from jax.experimental import pallas as pl
from jax.experimental.pallas import tpu as pltpu

F32 = jnp.float32
BF16 = jnp.bfloat16

D_MODEL = 1024
NORM_EPS = 1e-6
POOL_WINDOWS = (2, 4, 8, 16)
POOL_GROUP = D_MODEL // len(POOL_WINDOWS)
RWKV_HEAD = 64
RWKV_HEADS = D_MODEL // RWKV_HEAD
DECAY_SCALE = math.exp(-0.5)
LNX_EPS = 64e-5
DECAY_LORA = 64
PEER_HEADS = 8
PEER_KEYS = 128
PEER_EXPERTS = PEER_KEYS * PEER_KEYS
PEER_TOPK = 16
INV_SQRT2 = 0.7071067811865476

HALO = 8
VMEM_LIMIT = 56 * 1024 * 1024

POOL_TILE = 512
PROJ_TILE = 256
POST_TILE = 256
ROUTER_TILE = 128
DENSE_TILE = 1024
DENSE_EBLK = 1024
DENSE_SUB = 256
SCAN_CHUNK = 64
NTOP = PEER_TOPK + 1
CAND_COUNTS = tuple(min(NTOP, NTOP // (i + 1)) for i in range(NTOP))
CAND_ROWS = -(-sum(CAND_COUNTS) // 8) * 8


def _params(sem):
    return pltpu.CompilerParams(dimension_semantics=sem, vmem_limit_bytes=VMEM_LIMIT)


def _split(w):
    hi = w.astype(BF16)
    lo = (w - hi.astype(F32)).astype(BF16)
    return hi, lo


def _dot(a, b, dims=(((1,), (0,)), ((), ()))):
    return lax.dot_general(a, b, dims, preferred_element_type=F32)


def _dot3(a, b_hi, b_lo, dims=(((1,), (0,)), ((), ()))):
    a_hi, a_lo = _split(a)
    return _dot(a_hi, b_hi, dims) + (_dot(a_lo, b_hi, dims) + _dot(a_hi, b_lo, dims))


def _dot2(a, b_exact, dims=(((1,), (0,)), ((), ()))):
    a_hi, a_lo = _split(a)
    return _dot(a_hi, b_exact, dims) + _dot(a_lo, b_exact, dims)


def _dot3f(a, b, dims=(((1,), (0,)), ((), ()))):
    b_hi, b_lo = _split(b)
    return _dot3(a, b_hi, b_lo, dims)


def _dot1f(a, b, dims=(((1,), (0,)), ((), ()))):
    return _dot(a.astype(BF16), b.astype(BF16), dims)


NT = (((1,), (1,)), ((), ()))
TN = (((0,), (0,)), ((), ()))


def _rmsnorm(x, g):
    return x * lax.rsqrt(jnp.mean(x * x, axis=-1, keepdims=True) + NORM_EPS) * g


def _tile_flags(seq_lens, tile):
    first, last = [], []
    for s in seq_lens:
        n = s // tile
        assert n * tile == s
        first += [1] + [0] * (n - 1)
        last += [0] * (n - 1) + [1]
    return jnp.asarray(np.array([first, last], np.int32))


def _halo_specs(tile, n_tok):
    r = tile // HALO
    nb = n_tok // HALO
    return [
        pl.BlockSpec((tile, D_MODEL), lambda i, f: (i, 0)),
        pl.BlockSpec((HALO, D_MODEL), lambda i, f: (jnp.maximum(i * r - 1, 0), 0)),
        pl.BlockSpec((HALO, D_MODEL), lambda i, f: (jnp.minimum((i + 1) * r, nb - 1), 0)),
    ]


def _fill_extended(xe_ref, u, up, un, first, last, tile):
    xe_ref[0:HALO, :] = jnp.where(first != 0, 0.0, up)
    xe_ref[HALO:HALO + tile, :] = u
    xe_ref[HALO + tile:2 * HALO + tile, :] = jnp.where(last != 0, 0.0, un)


def _const_spec(shape):
    nd = len(shape)
    return pl.BlockSpec(shape, lambda *_: (0,) * nd, pipeline_mode=pl.Buffered(1))


def _pool_kernel(flags_ref, x_ref, xp_ref, xn_ref, g_ref, pwh_ref, pwl_ref, ps_ref, o_ref, xe_ref):
    i = pl.program_id(0)
    tile = x_ref.shape[0]
    first = flags_ref[0, i]
    last = flags_ref[1, i]
    g = g_ref[...]
    x = x_ref[...]
    u = _rmsnorm(x, g)
    _fill_extended(xe_ref, u, _rmsnorm(xp_ref[...], g), _rmsnorm(xn_ref[...], g), first, last, tile)
    row = lax.broadcasted_iota(jnp.int32, (tile, 1), 0)
    for gi, w in enumerate(POOL_WINDOWS):
        sl = slice(gi * POOL_GROUP, (gi + 1) * POOL_GROUP)
        acc = xe_ref[pl.ds(HALO - w // 2, tile), sl]
        for j in range(-w // 2 + 1, w // 2):
            acc = acc + xe_ref[pl.ds(HALO + j, tile), sl]
        lo_clip = jnp.where(first != 0, jnp.maximum(w // 2 - row, 0), 0)
        hi_clip = jnp.where(last != 0, jnp.maximum(row + w // 2 - tile, 0), 0)
        cnt = (w - lo_clip - hi_clip).astype(F32)
        diff = acc / cnt - u[:, sl]
        y = _dot3(diff, pwh_ref[gi], pwl_ref[gi])
        o_ref[:, sl] = x[:, sl] + y * ps_ref[:, sl]


def _pool_layer(h, flags, norm_g, pool_w, pool_scale):
    n_tok = h.shape[0]
    tile = POOL_TILE
    pwh, pwl = _split(pool_w)
    grid_spec = pltpu.PrefetchScalarGridSpec(
        num_scalar_prefetch=1,
        grid=(n_tok // tile,),
        in_specs=_halo_specs(tile, n_tok) + [
            _const_spec((1, D_MODEL)), _const_spec(pwh.shape), _const_spec(pwl.shape),
            _const_spec((1, D_MODEL)),
        ],
        out_specs=pl.BlockSpec((tile, D_MODEL), lambda i, f: (i, 0)),
        scratch_shapes=[pltpu.VMEM((tile + 2 * HALO, D_MODEL), F32)],
    )
    return pl.pallas_call(
        _pool_kernel,
        grid_spec=grid_spec,
        out_shape=jax.ShapeDtypeStruct((n_tok, D_MODEL), F32),
        compiler_params=_params(("parallel",)),
        name="pool",
    )(flags, h, h, h, norm_g.reshape(1, -1), pwh, pwl, pool_scale.reshape(1, -1))


def _top_values(cur, n, out_ref, slot):
    for r in range(n):
        mx = jnp.max(cur, axis=0, keepdims=True)
        out_ref[slot, r:r + 1, :] = mx
        cur = jnp.where(cur == mx, -jnp.inf, cur)


def _sort_network(n):
    pairs = []

    def merge(lo, hi, r):
        step = r * 2
        if step < hi - lo:
            merge(lo, hi, step)
            merge(lo + r, hi, step)
            pairs.extend((i, i + r) for i in range(lo + r, hi - r, step))
        else:
            pairs.append((lo, lo + r))

    def sort(lo, hi):
        if hi - lo >= 1:
            mid = lo + (hi - lo) // 2
            sort(lo, mid)
            sort(mid + 1, hi)
            merge(lo, hi, 1)

    sort(0, n - 1)
    return pairs


def _top_values_sorted(s, n, out_ref, slot):
    groups = PEER_KEYS // 8
    v = [s[8 * k:8 * (k + 1), :] for k in range(groups)]
    for i, j in _sort_network(groups):
        v[i], v[j] = jnp.maximum(v[i], v[j]), jnp.minimum(v[i], v[j])
    for r in range(n):
        mx = jnp.max(v[0], axis=0, keepdims=True)
        out_ref[slot, r:r + 1, :] = mx
        hit = v[0] == mx
        for k in range(min(groups, n - r - 1)):
            v[k] = jnp.where(hit, v[k + 1] if k + 1 < groups else -jnp.inf, v[k])


def _bf16_pair_words(x):
    bits = pltpu.bitcast(x.astype(BF16).astype(F32), jnp.uint32)
    return bits | lax.shift_right_logical(bits, jnp.uint32(16))


def _router_kernel(h_ref, g_ref, qwh_ref, qwl_ref, skh_ref, skl_ref,
                   hnT_ref, cnt_ref, e1_ref, rank_ref, e2_ref, s_ref, tops_ref, cand_ref):
    hn = _rmsnorm(h_ref[...], g_ref[...])
    hnT_ref[...] = hn.T.astype(BF16)
    q = _dot3(hn, qwh_ref[...], qwl_ref[...])
    for hp in range(2 * PEER_HEADS):
        qs = q[:, hp * PEER_KEYS:(hp + 1) * PEER_KEYS]
        qh, ql = _split(qs)
        skh = skh_ref[hp]
        s_ref[hp] = _dot(skh, qh, NT) + (_dot(skl_ref[hp], qh, NT) + _dot(skh, ql, NT))
    for h in range(PEER_HEADS):
        s1 = s_ref[2 * h]
        s2 = s_ref[2 * h + 1]
        _top_values_sorted(s1, NTOP, tops_ref, 0)
        _top_values_sorted(s2, NTOP, tops_ref, 1)
        cand_ref[...] = jnp.full(cand_ref.shape, -jnp.inf, F32)
        off = 0
        for i, n in enumerate(CAND_COUNTS):
            cand_ref[0, off:off + n, :] = tops_ref[0, i:i + 1, :] + tops_ref[1, 0:n, :]
            off += n
        m1 = tops_ref[0, 0:1, :]
        m2 = tops_ref[1, 0:1, :]
        _top_values(cand_ref[0], NTOP, tops_ref, 2)
        top = tops_ref[2, 0:PEER_TOPK, :]
        z = jnp.sum(jnp.exp(top - top[0:1, :]), axis=0, keepdims=True)
        tau = 0.5 * (tops_ref[2, PEER_TOPK - 1:PEER_TOPK, :] + tops_ref[2, PEER_TOPK:PEER_TOPK + 1, :])
        thr = tau - s1
        cnt = jnp.zeros_like(s1)
        rank = jnp.zeros_like(s2)
        for k in range(NTOP):
            tk = tops_ref[1, k:k + 1, :]
            cnt = jnp.where(tk >= thr, float(k + 1), cnt)
            rank = jnp.where(tk > s2, float(k + 1), rank)
        cnt_ref[h] = _bf16_pair_words(cnt)
        e1_ref[h] = _bf16_pair_words(jnp.exp(s1 - m1) * (1.0 / z))
        e2 = jnp.exp(s2 - m2)
        for tc in range(rank_ref.shape[1]):
            cols = slice(tc * 128, (tc + 1) * 128)
            rank_ref[h, tc] = pltpu.bitcast(rank[:, cols].astype(BF16), jnp.uint32)
            e2_ref[h, tc] = pltpu.bitcast(e2[:, cols].astype(BF16), jnp.uint32)


def _peer_router(h, norm_g, q_w, subkeys):
    n_tok = h.shape[0]
    tile = ROUTER_TILE
    qwh, qwl = _split(q_w)
    sk = subkeys.reshape(2 * PEER_HEADS, PEER_KEYS, PEER_KEYS)
    skh, skl = _split(sk)
    rows = pl.BlockSpec((PEER_HEADS, PEER_KEYS, tile), lambda i: (0, 0, i))
    rows_shape = jax.ShapeDtypeStruct((PEER_HEADS, PEER_KEYS, n_tok), jnp.uint32)
    tiles = pl.BlockSpec((PEER_HEADS, tile // 128, PEER_KEYS // 2, 128), lambda i: (0, i, 0, 0))
    tiles_shape = jax.ShapeDtypeStruct((PEER_HEADS, n_tok // 128, PEER_KEYS // 2, 128), jnp.uint32)
    return pl.pallas_call(
        _router_kernel,
        grid=(n_tok // tile,),
        in_specs=[
            pl.BlockSpec((tile, D_MODEL), lambda i: (i, 0)),
            _const_spec((1, D_MODEL)),
            _const_spec(qwh.shape), _const_spec(qwl.shape),
            _const_spec(skh.shape), _const_spec(skl.shape),
        ],
        out_specs=[pl.BlockSpec((D_MODEL, tile), lambda i: (0, i)), rows, rows, tiles, tiles],
        out_shape=[jax.ShapeDtypeStruct((D_MODEL, n_tok), BF16), rows_shape, rows_shape, tiles_shape, tiles_shape],
        scratch_shapes=[pltpu.VMEM((2 * PEER_HEADS, PEER_KEYS, tile), F32),
                        pltpu.VMEM((3, 24, tile), F32), pltpu.VMEM((1, CAND_ROWS, tile), F32)],
        compiler_params=_params(("parallel",)),
        name="peer_router",
    )(h, norm_g.reshape(1, -1), qwh, qwl, skh, skl)


def _rows_as_bf16(words):
    return pltpu.bitcast(jnp.broadcast_to(words, (PEER_KEYS // 2, words.shape[1])), BF16)


def _dense_kernel(h_ref, hnT_ref, cnt_ref, e1_ref, rank_ref, e2_ref, u_ref, u_next_ref, vT_ref, vT_prev_ref, gf_ref,
                  o_ref, acc_ref, act_ref, p_ref, *, final_norm):
    j = pl.program_id(1)
    eblk, tile = u_ref.shape[0], hnT_ref.shape[1]
    n_sub = eblk // DENSE_SUB
    hnT = hnT_ref[...]

    @pl.when(j == 0)
    def _():
        acc_ref[...] = jnp.zeros(acc_ref.shape, F32)
        p_ref[1] = jnp.zeros(p_ref.shape[1:], BF16)
        act_ref[0] = _dot(u_ref[0:DENSE_SUB, :], hnT)

    for sb in range(n_sub):
        r0 = sb * DENSE_SUB
        cur, nxt = sb % 2, (sb + 1) % 2
        u_rows = u_ref[r0 + DENSE_SUB:r0 + 2 * DENSE_SUB, :] if sb + 1 < n_sub else u_next_ref[...]
        v_cols = vT_ref[sb - 1] if sb > 0 else vT_prev_ref[0]
        for half in range(tile // 256):
            hc = slice(half * 256, (half + 1) * 256)
            act_ref[nxt, :, hc] = _dot(u_rows, hnT_ref[:, hc])
            acc_ref[:, hc] += _dot(v_cols, p_ref[nxt, :, hc])
        als = range(r0 // PEER_KEYS, (r0 + DENSE_SUB) // PEER_KEYS)
        for tc in range(tile // 128):
            cols = slice(tc * 128, (tc + 1) * 128)
            gates = [jnp.zeros((PEER_KEYS, 128), BF16) for _ in als]
            for h in range(PEER_HEADS):
                rank = pltpu.bitcast(rank_ref[h, tc], BF16)
                e2 = pltpu.bitcast(e2_ref[h, tc], BF16)
                for i, al in enumerate(als):
                    cnt_row = _rows_as_bf16(cnt_ref[h, al:al + 1, cols])
                    e1_row = _rows_as_bf16(e1_ref[h, al:al + 1, cols])
                    gates[i] = gates[i] + jnp.where(rank < cnt_row, e2 * e1_row, jnp.zeros((), BF16))
            for i, al in enumerate(als):
                rows = slice(al * PEER_KEYS - r0, (al + 1) * PEER_KEYS - r0)
                x = act_ref[cur, rows, cols]
                gelu = 0.5 * x * (1.0 + lax.erf(x * INV_SQRT2))
                p_ref[cur, rows, cols] = gates[i] * gelu.astype(BF16)

    @pl.when(j == pl.num_programs(1) - 1)
    def _():
        acc = acc_ref[...] + _dot(vT_ref[n_sub - 1], p_ref[(n_sub - 1) % 2])
        out = h_ref[...] + acc.T
        if final_norm:
            out = _rmsnorm(out, gf_ref[...])
        o_ref[...] = out


def _peer_dense(h, hnT, cnt, e1, rank, e2, u_bf, vT_bf, final_g, final_norm):
    n_tok = h.shape[0]
    tile, eblk = DENSE_TILE, DENSE_EBLK
    n_sub = eblk // DENSE_SUB
    last_sub = PEER_EXPERTS // DENSE_SUB - 1
    assert n_sub % 2 == 0
    rows = pl.BlockSpec((PEER_HEADS, eblk // PEER_KEYS, tile), lambda i, j: (0, j, i))
    full = pl.BlockSpec((PEER_HEADS, tile // 128, PEER_KEYS // 2, 128), lambda i, j: (0, i, 0, 0))
    return pl.pallas_call(
        functools.partial(_dense_kernel, final_norm=final_norm),
        grid=(n_tok // tile, PEER_EXPERTS // eblk),
        in_specs=[
            pl.BlockSpec((tile, D_MODEL), lambda i, j: (i, 0)),
            pl.BlockSpec((D_MODEL, tile), lambda i, j: (0, i)),
            rows, rows, full, full,
            pl.BlockSpec((eblk, D_MODEL), lambda i, j: (j, 0)),
            pl.BlockSpec((DENSE_SUB, D_MODEL), lambda i, j: (jnp.minimum((j + 1) * n_sub, last_sub), 0)),
            pl.BlockSpec((n_sub, D_MODEL, DENSE_SUB), lambda i, j: (j, 0, 0)),
            pl.BlockSpec((1, D_MODEL, DENSE_SUB), lambda i, j: (jnp.maximum(j * n_sub - 1, 0), 0, 0)),
            _const_spec((1, D_MODEL)),
        ],
        out_specs=pl.BlockSpec((tile, D_MODEL), lambda i, j: (i, 0)),
        out_shape=jax.ShapeDtypeStruct((n_tok, D_MODEL), F32),
        scratch_shapes=[
            pltpu.VMEM((D_MODEL, tile), F32),
            pltpu.VMEM((2, DENSE_SUB, tile), F32),
            pltpu.VMEM((2, DENSE_SUB, tile), BF16),
        ],
        compiler_params=_params(("parallel", "arbitrary")),
        name="peer_dense",
    )(h, hnT, cnt, e1, rank, e2, u_bf, u_bf, vT_bf, vT_bf, final_g.reshape(1, -1))


def _peer_layer(h, norm_g, q_w, subkeys, u_tab, v_tab, final_g, final_norm):
    hnT, cnt, e1, rank, e2 = _peer_router(h, norm_g, q_w, subkeys)
    vT = v_tab.reshape(PEER_EXPERTS // DENSE_SUB, DENSE_SUB, D_MODEL).transpose(0, 2, 1).astype(BF16)
    return _peer_dense(h, hnT, cnt, e1, rank, e2, u_tab.astype(BF16), vT, final_g, final_norm)


def _head_sum(x, ind_ref, indT_ref):
    return _dot2(_dot2(x, ind_ref[...]), indT_ref[...])


def _proj_kernel(flags_ref, x_ref, xp_ref, xn_ref, g_ref, mu_ref, w_ref,
                 dw0_ref, dw1_ref, dw2_ref, a0_ref, a1_ref, a2_ref, g1_ref, g2_ref,
                 kk_ref_w, ka_ref, ind_ref, indT_ref,
                 r_ref, k_ref, v_ref, gate_ref, kko_ref, lw_ref, kd_ref, beta_ref, xe_ref):
    i = pl.program_id(0)
    tile = x_ref.shape[0]
    first = flags_ref[0, i]
    last = flags_ref[1, i]
    g = g_ref[...]
    u = _rmsnorm(x_ref[...], g)
    _fill_extended(xe_ref, u, _rmsnorm(xp_ref[...], g), _rmsnorm(xn_ref[...], g), first, last, tile)
    xx = 0.5 * (xe_ref[pl.ds(HALO - 1, tile), :] + xe_ref[pl.ds(HALO + 1, tile), :]) - u
    xr, xw, xk, xv, xa, xg = (u + xx * mu_ref[m:m + 1, :] for m in range(6))
    r = _dot(xr.astype(BF16), w_ref[0])
    k = _dot(xk.astype(BF16), w_ref[1])
    v = _dot(xv.astype(BF16), w_ref[2])
    r_ref[...] = r
    k_ref[...] = k
    v_ref[...] = v
    gate_ref[...] = _dot(jax.nn.sigmoid(_dot(xg.astype(BF16), g1_ref[...])).astype(BF16), g2_ref[...])
    kk = k * kk_ref_w[...]
    nrm = jnp.sqrt(_head_sum(kk * kk, ind_ref, indT_ref))
    kk = kk / jnp.maximum(nrm, 1e-12)
    kko_ref[...] = kk
    lora = DECAY_LORA
    tw = jnp.tanh(_dot(xw.astype(BF16), dw1_ref[...])).astype(BF16)
    aw = _dot(xa.astype(BF16), a1_ref[...]).astype(BF16)
    for d in range(2):
        z = dw0_ref[d:d + 1, :] + _dot(tw[:, d * lora:(d + 1) * lora], dw2_ref[d])
        lw_ref[d] = -DECAY_SCALE * jax.nn.sigmoid(z)
        a = jax.nn.sigmoid(a0_ref[d:d + 1, :] + _dot(aw[:, d * lora:(d + 1) * lora], a2_ref[d]))
        kd_ref[d] = k * (1.0 + (a - 1.0) * ka_ref[...])
        beta_ref[d] = kk * a


def _head_indicator():
    ind = (np.arange(D_MODEL)[:, None] // RWKV_HEAD == np.arange(RWKV_HEADS)[None, :]).astype(np.float32)
    pad = np.zeros((D_MODEL, 128), np.float32)
    pad[:, :RWKV_HEADS] = ind
    return jnp.asarray(pad, BF16), jnp.asarray(pad.T, BF16)


def _rwkv_proj(h, flags, norm_g, mu, w_rkv, dec_w0, dec_w1, dec_w2, icl_a0, icl_a1, icl_a2,
               gate_g1, gate_g2, k_k, k_a):
    n_tok = h.shape[0]
    tile = PROJ_TILE
    ind, indT = _head_indicator()
    assert dec_w1.shape[-1] == DECAY_LORA == icl_a1.shape[-1]
    side_by_side = lambda w: jnp.concatenate([w[0], w[1]], axis=1).astype(BF16)
    consts = [norm_g.reshape(1, -1), mu, w_rkv.astype(BF16), dec_w0, side_by_side(dec_w1), dec_w2.astype(BF16),
              icl_a0, side_by_side(icl_a1), icl_a2.astype(BF16), gate_g1.astype(BF16), gate_g2.astype(BF16),
              k_k.reshape(1, -1), k_a.reshape(1, -1), ind, indT]

    tok = pl.BlockSpec((tile, D_MODEL), lambda i, f: (i, 0))
    tok2 = pl.BlockSpec((2, tile, D_MODEL), lambda i, f: (0, i, 0))
    one = jax.ShapeDtypeStruct((n_tok, D_MODEL), F32)
    two = jax.ShapeDtypeStruct((2, n_tok, D_MODEL), F32)
    grid_spec = pltpu.PrefetchScalarGridSpec(
        num_scalar_prefetch=1,
        grid=(n_tok // tile,),
        in_specs=_halo_specs(tile, n_tok) + [_const_spec(a.shape) for a in consts],
        out_specs=[tok] * 5 + [tok2] * 3,
        scratch_shapes=[pltpu.VMEM((tile + 2 * HALO, D_MODEL), F32)],
    )
    return pl.pallas_call(
        _proj_kernel,
        grid_spec=grid_spec,
        out_shape=[one] * 5 + [two] * 3,
        compiler_params=_params(("parallel",)),
        name="rwkv_proj",
    )(flags, h, h, h, *consts)


def _scan_prep(rev, r_ref, lw_ref, kd_ref, kk_ref, beta_ref, incl):
    L = r_ref.shape[0]
    lw = lw_ref[...]
    lw_hi = lw.astype(BF16)
    lw_r = lw - lw_hi.astype(F32)
    lw_mid = lw_r.astype(BF16)
    lw_lo = (lw_r - lw_mid.astype(F32)).astype(BF16)
    inc_bf = incl.astype(BF16)
    cs = _dot(inc_bf, lw_hi) + (_dot(inc_bf, lw_mid) + _dot(inc_bf, lw_lo))
    end = 0 if rev else L - 1
    cs_end = cs[end:end + 1, :]
    g_inv = jnp.exp(-cs)
    g_rest = jnp.exp(cs_end - cs)
    beta = beta_ref[...]
    kd = kd_ref[...]
    return dict(
        alpha_b=-kk_ref[...] * jnp.exp(cs - lw), r_b=r_ref[...] * jnp.exp(cs),
        beta_t=beta * g_inv, k_t=kd * g_inv, beta_h=beta * g_rest, k_h=kd * g_rest,
        g_end=jnp.exp(cs_end))


def _scan_kernel(flags_ref, rf, lwf, kdf, vf, kkf, bf, rb, lwb, kdb, vb, kkb, bb, yf_ref, yb_ref, stf_ref, stb_ref):
    c = pl.program_id(0)
    n = pl.num_programs(0)
    L = rf.shape[0]
    N = RWKV_HEAD
    row = lax.broadcasted_iota(jnp.int32, (L, L), 0)
    col = lax.broadcasted_iota(jnp.int32, (L, L), 1)
    eye = (row == col).astype(F32)
    incl = (row >= col, row <= col)
    strict = (row > col, row < col)
    prep = (_scan_prep(False, rf, lwf, kdf, kkf, bf, incl[0]), _scan_prep(True, rb, lwb, kdb, kkb, bb, incl[1]))
    v_all = (vf[...], vb[...])
    reset = (flags_ref[0, c], flags_ref[1, n - 1 - c])
    st_refs = (stf_ref, stb_ref)
    y_refs = (yf_ref, yb_ref)
    chains = [(d, h) for h in range(RWKV_HEADS) for d in range(2)]

    def cols(d, h, name):
        return prep[d][name][:, h * N:(h + 1) * N]

    t0 = [jnp.where(reset[d] != 0, 0.0, st_refs[d][h]) for d, h in chains]
    x2 = [jnp.concatenate([cols(d, h, "alpha_b"), cols(d, h, "r_b")], axis=0) for d, h in chains]
    y2 = [jnp.concatenate([cols(d, h, "beta_t"), cols(d, h, "k_t")], axis=0) for d, h in chains]
    aa = [_dot1f(a, b, NT) for a, b in zip(x2, y2)]
    xt = [_dot1f(a, b) for a, b in zip(x2, t0)]
    a_ab = [jnp.where(strict[d], m[0:L, 0:L], 0.0) for (d, h), m in zip(chains, aa)]
    a_ak = [jnp.where(strict[d], m[0:L, L:2 * L], 0.0) for (d, h), m in zip(chains, aa)]
    a_r = [jnp.concatenate([jnp.where(incl[d], m[L:2 * L, 0:L], 0.0), jnp.where(incl[d], m[L:2 * L, L:2 * L], 0.0)],
                           axis=1) for (d, h), m in zip(chains, aa)]
    vh = [v_all[d][:, h * N:(h + 1) * N] for d, h in chains]
    rhs = [x[0:L] + _dot1f(a, b) for x, a, b in zip(xt, a_ak, vh)]
    minv = [eye + a for a in a_ab]
    ap = a_ab
    for _ in range(int(math.log2(L)) - 1):
        ap = [_dot1f(a, a) for a in ap]
        minv = [m + _dot1f(m, a) for m, a in zip(minv, ap)]
    u = [_dot1f(m, b) for m, b in zip(minv, rhs)]
    uv = [jnp.concatenate([a, b], axis=0) for a, b in zip(u, vh)]
    y = [x[L:2 * L] + _dot1f(a, b) for x, a, b in zip(xt, a_r, uv)]
    bk = [jnp.concatenate([cols(d, h, "beta_h"), cols(d, h, "k_h")], axis=0) for d, h in chains]
    t1 = [_dot1f(a, b, TN) for a, b in zip(bk, uv)]
    for i, (d, h) in enumerate(chains):
        y_refs[d][:, h * N:(h + 1) * N] = y[i]
        st_refs[d][h] = t0[i] * prep[d]["g_end"][:, h * N:(h + 1) * N].T + t1[i]


def _rwkv_scan(flags, r, lw, kd, v, kk, beta):
    n_tok = r.shape[0]
    L = SCAN_CHUNK
    n = n_tok // L
    fwd = pl.BlockSpec((L, D_MODEL), lambda c, f: (c, 0))
    bwd = pl.BlockSpec((L, D_MODEL), lambda c, f: (n - 1 - c, 0))

    def dspec(d, rev):
        return pl.BlockSpec((None, L, D_MODEL), (lambda c, f: (d, n - 1 - c, 0)) if rev else (lambda c, f: (d, c, 0)))

    grid_spec = pltpu.PrefetchScalarGridSpec(
        num_scalar_prefetch=1,
        grid=(n,),
        in_specs=[fwd, dspec(0, False), dspec(0, False), fwd, fwd, dspec(0, False),
                  bwd, dspec(1, True), dspec(1, True), bwd, bwd, dspec(1, True)],
        out_specs=[fwd, bwd],
        scratch_shapes=[pltpu.VMEM((RWKV_HEADS, RWKV_HEAD, RWKV_HEAD), F32)] * 2,
    )
    one = jax.ShapeDtypeStruct((n_tok, D_MODEL), F32)
    return pl.pallas_call(
        _scan_kernel,
        grid_spec=grid_spec,
        out_shape=[one, one],
        compiler_params=_params(("arbitrary",)),
        name="rwkv_scan",
    )(flags, r, lw, kd, v, kk, beta, r, lw, kd, v, kk, beta)


def _post_kernel(h_ref, yf_ref, yb_ref, r_ref, k_ref, v_ref, g_ref, rk_ref, lg_ref, lb_ref,
                 wo_ref, ind_ref, indT_ref, o_ref):
    y = yf_ref[...] + yb_ref[...]
    inv_n = 1.0 / RWKV_HEAD
    mean = _head_sum(y, ind_ref, indT_ref) * inv_n
    yc = y - mean
    var = _head_sum(yc * yc, ind_ref, indT_ref) * inv_n
    yn = yc * lax.rsqrt(var + LNX_EPS) * lg_ref[...] + lb_ref[...]
    bonus = _head_sum(r_ref[...] * k_ref[...] * rk_ref[...], ind_ref, indT_ref) * v_ref[...]
    out = _dot(((yn + bonus) * g_ref[...]).astype(BF16), wo_ref[...])
    o_ref[...] = h_ref[...] + out


def _rwkv_post(h, yf, yb, r, k, v, g, r_k, lnx_g, lnx_b, w_o):
    n_tok = h.shape[0]
    tile = POST_TILE
    ind, indT = _head_indicator()
    tok = pl.BlockSpec((tile, D_MODEL), lambda i: (i, 0))
    consts = [r_k.reshape(1, -1), lnx_g.reshape(1, -1), lnx_b.reshape(1, -1), w_o.astype(BF16), ind, indT]
    return pl.pallas_call(
        _post_kernel,
        grid=(n_tok // tile,),
        in_specs=[tok] * 7 + [_const_spec(a.shape) for a in consts],
        out_specs=tok,
        out_shape=jax.ShapeDtypeStruct((n_tok, D_MODEL), F32),
        compiler_params=_params(("parallel",)),
        name="rwkv_post",
    )(h, yf, yb, r, k, v, g, *consts)


def _rwkv_layer(h, seq_lens, norm_g, mu, w_rkv, w_o, dec_w0, dec_w1, dec_w2, icl_a0, icl_a1, icl_a2,
                gate_g1, gate_g2, k_k, k_a, r_k, lnx_g, lnx_b):
    r, k, v, g, kk, lw, kd, beta = _rwkv_proj(
        h, _tile_flags(seq_lens, PROJ_TILE), norm_g, mu, w_rkv, dec_w0, dec_w1, dec_w2,
        icl_a0, icl_a1, icl_a2, gate_g1, gate_g2, k_k, k_a)
    yf, yb = _rwkv_scan(_tile_flags(seq_lens, SCAN_CHUNK), r, lw, kd, v, kk, beta)
    return _rwkv_post(h, yf, yb, r, k, v, g, r_k, lnx_g, lnx_b, w_o)


def _trunk(h, seq_lens, norm_mix, norm_ffn, norm_final, pool_w, pool_scale,
           rwkv_mu, rwkv_w_rkv, rwkv_w_o, rwkv_dec_w0, rwkv_dec_w1, rwkv_dec_w2,
           rwkv_icl_a0, rwkv_icl_a1, rwkv_icl_a2, rwkv_gate_g1, rwkv_gate_g2,
           rwkv_k_k, rwkv_k_a, rwkv_r_k, rwkv_lnx_g, rwkv_lnx_b,
           peer_q, peer_subkeys, peer_u, peer_v):
    depth = norm_mix.shape[0]
    for i in range(depth):
        j = i // 2
        if i % 2 == 0:
            h = _pool_layer(h, _tile_flags(seq_lens, POOL_TILE), norm_mix[i], pool_w[j], pool_scale[j])
        else:
            h = _rwkv_layer(h, seq_lens, norm_mix[i], rwkv_mu[j], rwkv_w_rkv[j], rwkv_w_o[j],
                            rwkv_dec_w0[j], rwkv_dec_w1[j], rwkv_dec_w2[j], rwkv_icl_a0[j],
                            rwkv_icl_a1[j], rwkv_icl_a2[j], rwkv_gate_g1[j], rwkv_gate_g2[j],
                            rwkv_k_k[j], rwkv_k_a[j], rwkv_r_k[j].reshape(-1), rwkv_lnx_g[j], rwkv_lnx_b[j])
        h = _peer_layer(h, norm_ffn[i], peer_q[i], peer_subkeys[i], peer_u[i], peer_v[i],
                        norm_final, final_norm=(i == depth - 1))
    return h


def kernel(x_prompt, x_sample, norm_mix, norm_ffn, norm_final, pool_w, pool_scale, rwkv_mu, rwkv_w_rkv, rwkv_w_o, rwkv_dec_w0, rwkv_dec_w1, rwkv_dec_w2, rwkv_icl_a0, rwkv_icl_a1, rwkv_icl_a2, rwkv_gate_g1, rwkv_gate_g2, rwkv_k_k, rwkv_k_a, rwkv_r_k, rwkv_lnx_g, rwkv_lnx_b, peer_q, peer_subkeys, peer_u, peer_v):
    bp, sp, d = x_prompt.shape
    bs, ss, _ = x_sample.shape
    seq_lens = (sp,) * bp + (ss,) * bs
    h = jnp.concatenate([x_prompt.reshape(-1, d), x_sample.reshape(-1, d)], axis=0)
    y = _trunk(h, seq_lens, norm_mix, norm_ffn, norm_final, pool_w, pool_scale,
               rwkv_mu, rwkv_w_rkv, rwkv_w_o, rwkv_dec_w0, rwkv_dec_w1, rwkv_dec_w2,
               rwkv_icl_a0, rwkv_icl_a1, rwkv_icl_a2, rwkv_gate_g1, rwkv_gate_g2,
               rwkv_k_k, rwkv_k_a, rwkv_r_k, rwkv_lnx_g, rwkv_lnx_b,
               peer_q, peer_subkeys, peer_u, peer_v)
    n_p = bp * sp
    return (y[:n_p].reshape(bp, sp, d), y[n_p:].reshape(bs, ss, d))
```
